```python
import jax
import jax.numpy as jnp
from jax import lax
import numpy as np

D_MODEL = 1024
BATCH = 8
SEQ = 2048
DEPTH = 2

CHUNK = 64
MIX_WIDTH = D_MODEL
M_WIDTH = MIX_WIDTH // 2
M_HEADS = 4
M_HD = M_WIDTH // M_HEADS
M_CONV = 4
R_WIDTH = MIX_WIDTH - M_WIDTH
R_HD = 64
R_HEADS = R_WIDTH // R_HD
R_DECAY_LR = 64
R_A_LR = 64
R_G_LR = 128
M_SIZES = (M_WIDTH, M_WIDTH, M_WIDTH, M_WIDTH, M_HEADS, M_HEADS)
R_SIZES = (R_WIDTH, R_WIDTH, R_WIDTH, R_DECAY_LR, R_A_LR, R_G_LR)
M_COLS = sum(M_SIZES)
R_COLS = sum(R_SIZES)
IN_COLS = M_COLS + R_COLS
X_HEADS = 4
X_HD = D_MODEL // X_HEADS
N_MEM = 256
N_EXPERTS = 32
TOP_K = 4
D_FF = D_MODEL
SWIGLU_LIMIT = 7.0
SWIGLU_ALPHA = 1.702
MOE_BLOCK = 128
LN_EPS = 1e-5
HEAD_NORM_EPS = 1e-5
RWKV_GN_EPS = 64e-5
DN_ALPHA = (2 * DEPTH) ** 0.25
DN_BETA = (8 * DEPTH) ** -0.25

kernel_name = 'hybrid_mlstm_rwkv7_moe_encoder'


def split_cols(p, sizes):
    cuts = [int(c) for c in np.cumsum(sizes)[:-1]]
    return jnp.split(p, cuts, axis=-1)


def layer_norm(x, g, b):
    xf = x.astype(jnp.float32)
    mu = jnp.mean(xf, axis=-1, keepdims=True)
    var = jnp.mean(jnp.square(xf - mu), axis=-1, keepdims=True)
    y = (xf - mu) * lax.rsqrt(var + LN_EPS) * g.astype(jnp.float32) + b.astype(jnp.float32)
    return y.astype(x.dtype)


def head_standardize(x, eps):
    mu = jnp.mean(x, axis=-1, keepdims=True)
    var = jnp.mean(jnp.square(x - mu), axis=-1, keepdims=True)
    return (x - mu) * lax.rsqrt(var + eps)


def causal_dwconv(x, w, b):
    k_w, c = w.shape
    y = lax.conv_general_dilated(x, w[:, None, :].astype(x.dtype), window_strides=(1,),
                                 padding=[(k_w - 1, 0)],
                                 dimension_numbers=('NWC', 'WIO', 'NWC'),
                                 feature_group_count=c)
    return y + b


def token_shift(p, mu):
    prev = jnp.pad(p, ((0, 0), (1, 0), (0, 0)))[:, :-1]
    return p + (prev - p) * mu


def mlstm_chunkwise(q, k, v, i_pre, f_pre):
    f32 = jnp.float32
    B, S, H, D = q.shape
    NC = S // CHUNK
    blk = lambda t: t.astype(f32).reshape(B, NC, CHUNK, H, D).transpose(0, 3, 1, 2, 4)
    gblk = lambda t: t.astype(f32).reshape(B, NC, CHUNK, H).transpose(0, 3, 1, 2)
    q, k, v = blk(q), blk(k) * (D ** -0.5), blk(v)
    ig = gblk(i_pre)
    b = jnp.cumsum(jax.nn.log_sigmoid(gblk(f_pre)), axis=-1)
    b_last = b[..., -1]
    src = b_last[..., None] - b + ig
    m_loc = jnp.max(src, axis=-1)
    w_src = jnp.exp(src - m_loc[..., None])
    C_loc = jnp.einsum('bhcsv,bhcsk->bhcvk', v * w_src[..., None], k)
    n_loc = jnp.einsum('bhcs,bhcsk->bhck', w_src, k)

    def step(carry, inp):
        C, n, m = carry
        C_l, n_l, m_l, bl = inp
        m_new = jnp.maximum(bl + m, m_l)
        a = jnp.exp(bl + m - m_new)
        c = jnp.exp(m_l - m_new)
        C_new = a[..., None, None] * C + c[..., None, None] * C_l
        n_new = a[..., None] * n + c[..., None] * n_l
        return (C_new, n_new, m_new), (C, n, m)

    init = (jnp.zeros((B, H, D, D), f32), jnp.zeros((B, H, D), f32), jnp.zeros((B, H), f32))
    xs = (jnp.moveaxis(C_loc, 2, 0), jnp.moveaxis(n_loc, 2, 0),
          jnp.moveaxis(m_loc, 2, 0), jnp.moveaxis(b_last, 2, 0))
    _, (C_prev, n_prev, m_prev) = lax.scan(step, init, xs)
    C_prev = jnp.moveaxis(C_prev, 0, 2)
    n_prev = jnp.moveaxis(n_prev, 0, 2)
    m_prev = jnp.moveaxis(m_prev, 0, 2)

    g_inter = b + m_prev[..., None]
    causal = jnp.tril(jnp.ones((CHUNK, CHUNK), dtype=bool))
    d_mat = jnp.where(causal, b[..., :, None] - b[..., None, :] + ig[..., None, :], -jnp.inf)
    m_t = jnp.maximum(g_inter, jnp.max(d_mat, axis=-1))
    s = jnp.einsum('bhctd,bhcsd->bhcts', q, k) * jnp.exp(d_mat - m_t[..., None])
    w_inter = jnp.exp(g_inter - m_t)
    num = jnp.einsum('bhcts,bhcsv->bhctv', s, v) + w_inter[..., None] * jnp.einsum('bhcvk,bhctk->bhctv', C_prev, q)
    den = jnp.sum(s, axis=-1) + w_inter * jnp.einsum('bhck,bhctk->bhct', n_prev, q)
    h = num / jnp.maximum(jnp.abs(den), jnp.exp(-m_t))[..., None]
    return h.transpose(0, 2, 3, 1, 4).reshape(B, S, H, D)


def rwkv7_scan(r, w_log, k, v, kk, a):
    f32 = jnp.float32
    B, S, H, D = r.shape
    decay = jnp.exp(-jnp.exp(w_log.astype(f32)))

    def step(state, inp):
        r_t, d_t, k_t, v_t, kk_t, a_t = inp
        sa = jnp.einsum('bhvk,bhk->bhv', state, -kk_t)
        state = (state * d_t[:, :, None, :] + sa[..., None] * (kk_t * a_t)[:, :, None, :]
                 + v_t[..., None] * k_t[:, :, None, :])
        y = jnp.einsum('bhvk,bhk->bhv', state, r_t)
        return state, y

    xs = tuple(t.astype(f32).transpose(1, 0, 2, 3) for t in (r, decay, k, v, kk, a))
    _, y = lax.scan(step, jnp.zeros((B, H, D, D), f32), xs)
    return y.transpose(1, 0, 2, 3)


def hybrid_mixer(h, w_in, m_conv_w, m_conv_b, m_ig_b, m_fg_b, m_norm_g, r_mu, r_w0, r_w2,
                 r_a0, r_a2, r_g2, r_kk, r_ka, r_rk, r_gn_g, r_gn_b, w_out):
    f32 = jnp.float32
    B, S, _ = h.shape
    proj = h @ w_in
    m_cols, r_cols = proj[..., :M_COLS], proj[..., M_COLS:]

    mq, mk, mv, mo, mi, mf = split_cols(m_cols, M_SIZES)
    qk = jax.nn.silu(causal_dwconv(jnp.concatenate([mq, mk], axis=-1), m_conv_w, m_conv_b))
    mq, mk = qk[..., :M_WIDTH], qk[..., M_WIDTH:]
    hm = mlstm_chunkwise(mq.reshape(B, S, M_HEADS, M_HD), mk.reshape(B, S, M_HEADS, M_HD),
                         mv.reshape(B, S, M_HEADS, M_HD), mi + m_ig_b, mf + m_fg_b)
    hm = head_standardize(hm, HEAD_NORM_EPS).reshape(B, S, M_WIDTH) * m_norm_g
    hm = hm * jax.nn.sigmoid(mo.astype(f32))

    rr, rk, rv, rwl, ral, rgl = split_cols(token_shift(r_cols, r_mu).astype(f32), R_SIZES)
    w_log = -jax.nn.softplus(-(r_w0 + jnp.tanh(rwl) @ r_w2)) - 0.5
    a = jax.nn.sigmoid(r_a0 + ral @ r_a2)
    g = jax.nn.sigmoid(rgl) @ r_g2
    kk = (rk * r_kk).reshape(B, S, R_HEADS, R_HD)
    kk = kk / jnp.maximum(jnp.sqrt(jnp.sum(jnp.square(kk), axis=-1, keepdims=True)), 1e-12)
    rk = rk * (1.0 + (a - 1.0) * r_ka)
    r4, k4, v4, a4, w4 = [t.reshape(B, S, R_HEADS, R_HD) for t in (rr, rk, rv, a, w_log)]
    y = rwkv7_scan(r4, w4, k4, v4, kk, a4)
    y = head_standardize(y, RWKV_GN_EPS).reshape(B, S, R_WIDTH) * r_gn_g + r_gn_b
    bonus = jnp.sum(r4 * k4 * r_rk.reshape(R_HEADS, R_HD), axis=-1, keepdims=True) * v4
    hr = (y + bonus.reshape(B, S, R_WIDTH)) * g

    mixed = jnp.concatenate([hm, hr], axis=-1).astype(h.dtype)
    return mixed @ w_out


def cross_attention(h, mem, wq, wkv, wo):
    B, S, _ = h.shape
    M = mem.shape[1]
    q = (h @ wq).reshape(B, S, X_HEADS, X_HD)
    kv = (mem @ wkv).reshape(B, M, 2, X_HEADS, X_HD)
    k, v = kv[:, :, 0], kv[:, :, 1]
    s = jnp.einsum('bshd,bmhd->bhsm', q, k).astype(jnp.float32) * (X_HD ** -0.5)
    p = jax.nn.softmax(s, axis=-1).astype(h.dtype)
    o = jnp.einsum('bhsm,bmhd->bshd', p, v).reshape(B, S, X_HEADS * X_HD)
    return o @ wo


def moe_ffn(h, wr, br, w1, b1, w2, b2):
    B, S, Dm = h.shape
    xt = h.reshape(-1, Dm)
    N = xt.shape[0]
    logits = (xt @ wr).astype(jnp.float32) + br.astype(jnp.float32)
    top_v, top_i = lax.top_k(logits, TOP_K)
    gates = jax.nn.softmax(top_v, axis=-1)
    n_assign = N * TOP_K
    e_flat = top_i.reshape(-1)
    tok_flat = jnp.arange(n_assign) // TOP_K
    order = jnp.argsort(e_flat)
    e_sorted = e_flat[order]
    tok_sorted = tok_flat[order]
    gate_sorted = gates.reshape(-1)[order]
    counts = jnp.bincount(e_flat, length=N_EXPERTS)
    start = jnp.cumsum(counts) - counts
    padded = (counts + MOE_BLOCK - 1) // MOE_BLOCK * MOE_BLOCK
    pad_end = jnp.cumsum(padded)
    pad_start = pad_end - padded
    dest = pad_start[e_sorted] + (jnp.arange(n_assign) - start[e_sorted])
    n_rows = ((n_assign + MOE_BLOCK - 1) // MOE_BLOCK + N_EXPERTS) * MOE_BLOCK
    n_blocks = n_rows // MOE_BLOCK
    buf = jnp.zeros((n_rows, Dm), h.dtype).at[dest].set(xt[tok_sorted])
    block_e = jnp.minimum(jnp.searchsorted(pad_end, jnp.arange(n_blocks) * MOE_BLOCK, side='right'),
                          N_EXPERTS - 1)

    def expert_block(args):
        xb, e = args
        hu = xb @ w1[e] + b1[e]
        gate, up = hu[:, :D_FF], hu[:, D_FF:]
        gate = jnp.minimum(gate, SWIGLU_LIMIT)
        up = jnp.clip(up, -SWIGLU_LIMIT, SWIGLU_LIMIT)
        glu = gate * jax.nn.sigmoid(gate * SWIGLU_ALPHA)
        return ((up + 1.0) * glu) @ w2[e] + b2[e]

    y_buf = lax.map(expert_block, (buf.reshape(n_blocks, MOE_BLOCK, Dm), block_e)).reshape(n_rows, Dm)
    y = y_buf[dest] * gate_sorted[:, None].astype(y_buf.dtype)
    out = jnp.zeros((N, Dm), y.dtype).at[tok_sorted].add(y)
    return out.reshape(B, S, Dm).astype(h.dtype)


def setup_inputs(seed: int = 0) -> dict:
    key = jax.random.key(seed)
    ks = iter(jax.random.split(key, 64))
    f32 = jnp.float32
    L = DEPTH

    def nrm(shape, scale):
        return jax.random.normal(next(ks), shape, f32) * scale

    return {
        'x': nrm((BATCH, SEQ, D_MODEL), 1.0),
        'mem': nrm((BATCH, N_MEM, D_MODEL), 1.0),
        'ln0_g': 1.0 + nrm((D_MODEL,), 0.02),
        'ln0_b': nrm((D_MODEL,), 0.02),
        'w_in': nrm((L, D_MODEL, IN_COLS), D_MODEL ** -0.5),
        'm_conv_w': nrm((L, M_CONV, 2 * M_WIDTH), M_CONV ** -0.5),
        'm_conv_b': nrm((L, 2 * M_WIDTH), 0.02),
        'm_ig_b': nrm((L, M_HEADS), 0.1),
        'm_fg_b': jnp.linspace(3.0, 6.0, M_HEADS, dtype=f32)[None] + nrm((L, M_HEADS), 0.1),
        'm_norm_g': 1.0 + nrm((L, M_WIDTH), 0.02),
        'r_mu': jax.random.uniform(next(ks), (L, R_COLS), f32),
        'r_w0': jnp.linspace(-6.0, -1.0, R_WIDTH, dtype=f32)[None] + nrm((L, R_WIDTH), 0.1),
        'r_w2': nrm((L, R_DECAY_LR, R_WIDTH), 0.1),
        'r_a0': nrm((L, R_WIDTH), 0.1),
        'r_a2': nrm((L, R_A_LR, R_WIDTH), R_A_LR ** -0.5),
        'r_g2': nrm((L, R_G_LR, R_WIDTH), R_G_LR ** -0.5),
        'r_kk': 0.85 + nrm((L, R_WIDTH), 0.05),
        'r_ka': 1.0 + nrm((L, R_WIDTH), 0.05),
        'r_rk': nrm((L, R_WIDTH), 0.1),
        'r_gn_g': 1.0 + nrm((L, R_WIDTH), 0.02),
        'r_gn_b': nrm((L, R_WIDTH), 0.02),
        'w_out': nrm((L, MIX_WIDTH, D_MODEL), MIX_WIDTH ** -0.5 * DN_BETA),
        'ln1_g': 1.0 + nrm((L, D_MODEL), 0.02),
        'ln1_b': nrm((L, D_MODEL), 0.02),
        'x_wq': nrm((L, D_MODEL, D_MODEL), D_MODEL ** -0.5),
        'x_wkv': nrm((L, D_MODEL, 2 * D_MODEL), D_MODEL ** -0.5),
        'x_wo': nrm((L, D_MODEL, D_MODEL), D_MODEL ** -0.5 * DN_BETA),
        'ln2_g': 1.0 + nrm((L, D_MODEL), 0.02),
        'ln2_b': nrm((L, D_MODEL), 0.02),
        'moe_wr': nrm((L, D_MODEL, N_EXPERTS), D_MODEL ** -0.5),
        'moe_br': nrm((L, N_EXPERTS), 0.01),
        'moe_w1': nrm((L, N_EXPERTS, D_MODEL, 2 * D_FF), D_MODEL ** -0.5),
        'moe_b1': nrm((L, N_EXPERTS, 2 * D_FF), 0.01),
        'moe_w2': nrm((L, N_EXPERTS, D_FF, D_MODEL), D_FF ** -0.5 * DN_BETA),
        'moe_b2': nrm((L, N_EXPERTS, D_MODEL), 0.01),
        'ln3_g': 1.0 + nrm((L, D_MODEL), 0.02),
        'ln3_b': nrm((L, D_MODEL), 0.02),
    }


def reference(x, mem, ln0_g, ln0_b, w_in, m_conv_w, m_conv_b, m_ig_b, m_fg_b, m_norm_g,
              r_mu, r_w0, r_w2, r_a0, r_a2, r_g2, r_kk, r_ka, r_rk, r_gn_g, r_gn_b, w_out,
              ln1_g, ln1_b, x_wq, x_wkv, x_wo, ln2_g, ln2_b, moe_wr, moe_br, moe_w1, moe_b1,
              moe_w2, moe_b2, ln3_g, ln3_b):
    h = layer_norm(x, ln0_g, ln0_b)
    for l in range(DEPTH):
        mix = hybrid_mixer(h, w_in[l], m_conv_w[l], m_conv_b[l], m_ig_b[l], m_fg_b[l], m_norm_g[l],
                           r_mu[l], r_w0[l], r_w2[l], r_a0[l], r_a2[l], r_g2[l], r_kk[l], r_ka[l],
                           r_rk[l], r_gn_g[l], r_gn_b[l], w_out[l])
        h = layer_norm(DN_ALPHA * h + mix, ln1_g[l], ln1_b[l])
        h = layer_norm(DN_ALPHA * h + cross_attention(h, mem, x_wq[l], x_wkv[l], x_wo[l]),
                       ln2_g[l], ln2_b[l])
        h = layer_norm(DN_ALPHA * h + moe_ffn(h, moe_wr[l], moe_br[l], moe_w1[l], moe_b1[l],
                                              moe_w2[l], moe_b2[l]),
                       ln3_g[l], ln3_b[l])
    return h
```

```python
import functools

import jax
import jax.numpy as jnp
from jax import lax
from jax.experimental import pallas as pl
from jax.experimental.pallas import tpu as pltpu

F32 = jnp.float32
BF16 = jnp.bfloat16
HIGHEST = lax.Precision.HIGHEST

D_MODEL = 1024
DEPTH = 2
CHUNK = 64
M_WIDTH = 512
M_HEADS = 4
M_HD = 128
M_CONV = 4
R_WIDTH = 512
R_HD = 64
R_HEADS = 8
R_LR = 256
X_HEADS = 4
X_HD = 256
N_EXPERTS = 32
TOP_K = 4
D_FF = 1024
SWIGLU_LIMIT = 7.0
SWIGLU_ALPHA = 1.702
LN_EPS = 1e-5
HEAD_NORM_EPS = 1e-5
RWKV_GN_EPS = 64e-5
DN_ALPHA = (2 * DEPTH) ** 0.25

LANES = 128
M_COLS_PAD = 4 * M_WIDTH + LANES
R_COLS = 3 * R_WIDTH + R_LR
MOE_ROWS = 256
VMEM_LIMIT = 56 * 1024 * 1024


def _cparams(sem, vmem=VMEM_LIMIT):
    return pltpu.CompilerParams(dimension_semantics=sem, vmem_limit_bytes=vmem)


def _dot(a, b):
    return jnp.dot(a, b, preferred_element_type=F32)


def _dot_nt(a, b):
    return lax.dot_general(a, b, (((1,), (1,)), ((), ())), preferred_element_type=F32)


def _dot_hi(a, b):
    return jnp.dot(a, b, preferred_element_type=F32, precision=HIGHEST)


def _dot_nt_hi(a, b):
    return lax.dot_general(a, b, (((1,), (1,)), ((), ())), preferred_element_type=F32,
                           precision=HIGHEST)


def _layer_norm(x, g, b):
    mu = jnp.mean(x, axis=-1, keepdims=True)
    xc = x - mu
    var = jnp.mean(xc * xc, axis=-1, keepdims=True)
    return xc * lax.rsqrt(var + LN_EPS) * g + b


def _sigmoid(x):
    return 1.0 / (1.0 + jnp.exp(-x))


def _log_sigmoid(x):
    return jnp.minimum(x, 0.0) - jnp.log1p(jnp.exp(-jnp.abs(x)))


def _ln_kernel(x_ref, g_ref, b_ref, o_ref):
    o_ref[...] = _layer_norm(x_ref[...], g_ref[...], b_ref[...])


def _ln(x, g, b, tm=512):
    n, d = x.shape
    return pl.pallas_call(
        _ln_kernel,
        grid=(n // tm,),
        in_specs=[pl.BlockSpec((tm, d), lambda i: (i, 0)),
                  pl.BlockSpec((1, d), lambda i: (0, 0)),
                  pl.BlockSpec((1, d), lambda i: (0, 0))],
        out_specs=pl.BlockSpec((tm, d), lambda i: (i, 0)),
        out_shape=jax.ShapeDtypeStruct((n, d), F32),
        compiler_params=_cparams(("parallel",)),
    )(x, g.reshape(1, d), b.reshape(1, d))


def _proj_kernel(x_ref, wm_ref, wr_ref, om_ref, or_ref):
    x = x_ref[...].astype(BF16)
    om_ref[...] = _dot(x, wm_ref[...])
    or_ref[...] = _dot(x, wr_ref[...])


def _proj(h, wm, wr, tm=512):
    n, d = h.shape
    return pl.pallas_call(
        _proj_kernel,
        grid=(n // tm,),
        in_specs=[pl.BlockSpec((tm, d), lambda i: (i, 0)),
                  pl.BlockSpec(wm.shape, lambda i: (0, 0)),
                  pl.BlockSpec(wr.shape, lambda i: (0, 0))],
        out_specs=[pl.BlockSpec((tm, wm.shape[1]), lambda i: (i, 0)),
                   pl.BlockSpec((tm, wr.shape[1]), lambda i: (i, 0))],
        out_shape=[jax.ShapeDtypeStruct((n, wm.shape[1]), F32),
                   jax.ShapeDtypeStruct((n, wr.shape[1]), F32)],
        compiler_params=_cparams(("parallel",)),
    )(h, wm, wr)


def _mm_kernel(x_ref, w_ref, o_ref):
    o_ref[...] = _dot(x_ref[...].astype(BF16), w_ref[...]).astype(o_ref.dtype)


def _mm(x, w, out_dtype, tm=512):
    n, d = x.shape
    return pl.pallas_call(
        _mm_kernel,
        grid=(n // tm,),
        in_specs=[pl.BlockSpec((tm, d), lambda i: (i, 0)),
                  pl.BlockSpec(w.shape, lambda i: (0, 0))],
        out_specs=pl.BlockSpec((tm, w.shape[1]), lambda i: (i, 0)),
        out_shape=jax.ShapeDtypeStruct((n, w.shape[1]), out_dtype),
        compiler_params=_cparams(("parallel",)),
    )(x, w)


def _mix_out_kernel(hm_ref, hr_ref, h_ref, w_ref, g_ref, b_ref, o_ref):
    mix = _dot(hm_ref[...].astype(BF16), w_ref[:M_WIDTH, :])
    mix = mix + _dot(hr_ref[...].astype(BF16), w_ref[M_WIDTH:, :])
    o_ref[...] = _layer_norm(DN_ALPHA * h_ref[...] + mix, g_ref[...], b_ref[...])


def _mix_out(hm, hr, h, w, g, b, tm=512):
    n, d = h.shape
    return pl.pallas_call(
        _mix_out_kernel,
        grid=(n // tm,),
        in_specs=[pl.BlockSpec((tm, M_WIDTH), lambda i: (i, 0)),
                  pl.BlockSpec((tm, R_WIDTH), lambda i: (i, 0)),
                  pl.BlockSpec((tm, d), lambda i: (i, 0)),
                  pl.BlockSpec(w.shape, lambda i: (0, 0)),
                  pl.BlockSpec((1, d), lambda i: (0, 0)),
                  pl.BlockSpec((1, d), lambda i: (0, 0))],
        out_specs=pl.BlockSpec((tm, d), lambda i: (i, 0)),
        out_shape=jax.ShapeDtypeStruct((n, d), F32),
        compiler_params=_cparams(("parallel",)),
    )(hm, hr, h, w, g.reshape(1, d), b.reshape(1, d))


def _mlstm_kernel(qk_ref, v_ref, o_ref, gt_ref, cw_ref, cb_ref, gb_ref, ng_ref, out_ref,
                  ct_s, n_s, m_s, tail_s, *, n_chunks):
    L = CHUNK

    @pl.when(pl.program_id(1) == 0)
    def _():
        ct_s[...] = jnp.zeros_like(ct_s)
        n_s[...] = jnp.zeros_like(n_s)
        m_s[...] = jnp.zeros_like(m_s)
        tail_s[...] = jnp.zeros_like(tail_s)

    row = lax.broadcasted_iota(jnp.int32, (L, L), 0)
    col = lax.broadcasted_iota(jnp.int32, (L, L), 1)
    causal = row >= col
    tril_f = causal.astype(F32)
    lane = lax.broadcasted_iota(jnp.int32, (1, LANES), 1)

    def chunk(c, carry):
        r0 = pl.multiple_of(c * L, L)
        x = qk_ref[0, pl.ds(r0, L), :]
        cat = jnp.concatenate([tail_s[...], x], axis=0)
        acc = cb_ref[...] + cw_ref[M_CONV - 1:M_CONV, :] * x
        for j in range(M_CONV - 1):
            acc = acc + cw_ref[j:j + 1, :] * pltpu.roll(cat, M_CONV - 1 - j, axis=0)[8:, :]
        tail_s[...] = x[L - 8:, :]
        qk = acc * _sigmoid(acc)

        gx = gt_ref[0, pl.ds(r0, L), :] + gb_ref[...]
        bc = _dot_hi(tril_f, _log_sigmoid(gx))
        rows = jnp.where(lane < M_HEADS, gx, bc).T

        for h in range(M_HEADS):
            q = qk[:, h * M_HD:(h + 1) * M_HD]
            k = qk[:, M_WIDTH + h * M_HD:M_WIDTH + (h + 1) * M_HD] * (M_HD ** -0.5)
            v = v_ref[0, pl.ds(r0, L), h * M_HD:(h + 1) * M_HD]
            ig_c = gx[:, h:h + 1]
            b_c = bc[:, M_HEADS + h:M_HEADS + h + 1]
            ig_r = rows[h:h + 1, :]
            b_r = rows[M_HEADS + h:M_HEADS + h + 1, :]
            b_last = bc[L - 1:L, M_HEADS + h:M_HEADS + h + 1]
            m_prev = m_s[h, 0:1, 0:1]
            n_prev = n_s[h, 0:1, :]
            ct_prev = ct_s[h]

            qb = q.astype(BF16)
            kb = k.astype(BF16)
            vb = v.astype(BF16)

            g_inter = b_c + m_prev
            d_mat = jnp.where(causal, b_c - b_r + ig_r, -jnp.inf)
            m_t = jnp.maximum(g_inter, jnp.max(d_mat, axis=1, keepdims=True))
            s = _dot_nt(qb, kb) * jnp.exp(d_mat - m_t)
            w_inter = jnp.exp(g_inter - m_t)
            num = _dot(s.astype(BF16), vb) + w_inter * _dot(qb, ct_prev.astype(BF16))
            den = (jnp.sum(s, axis=1, keepdims=True)
                   + w_inter * jnp.sum(q * n_prev, axis=1, keepdims=True))
            hc = num / jnp.maximum(jnp.abs(den), jnp.exp(-m_t))
            mu = jnp.mean(hc, axis=1, keepdims=True)
            hcc = hc - mu
            var = jnp.mean(hcc * hcc, axis=1, keepdims=True)
            og = o_ref[0, pl.ds(r0, L), h * M_HD:(h + 1) * M_HD]
            out = hcc * lax.rsqrt(var + HEAD_NORM_EPS) * ng_ref[:, h * M_HD:(h + 1) * M_HD]
            out_ref[0, pl.ds(r0, L), h * M_HD:(h + 1) * M_HD] = out * _sigmoid(og)

            src = b_last - b_c + ig_c
            m_loc = jnp.max(src, axis=0, keepdims=True)
            kw = k * jnp.exp(src - m_loc)
            ct_loc = _dot(kw.T.astype(BF16), vb)
            n_loc = jnp.sum(kw, axis=0, keepdims=True)
            m_new = jnp.maximum(b_last + m_prev, m_loc)
            a = jnp.exp(b_last + m_prev - m_new)
            cc = jnp.exp(m_loc - m_new)
            ct_s[h] = a * ct_prev + cc * ct_loc
            n_s[h] = jnp.broadcast_to(a * n_prev + cc * n_loc, (8, M_HD))
            m_s[h] = jnp.broadcast_to(m_new, (8, LANES))
        return carry

    lax.fori_loop(0, n_chunks, chunk, 0)


def _mlstm(proj_m, cw, cb, gb, ng, ts=512):
    bsz, s, _ = proj_m.shape
    kern = functools.partial(_mlstm_kernel, n_chunks=ts // CHUNK)
    return pl.pallas_call(
        kern,
        grid=(bsz, s // ts),
        in_specs=[pl.BlockSpec((1, ts, 2 * M_WIDTH), lambda b, t: (b, t, 0)),
                  pl.BlockSpec((1, ts, M_WIDTH), lambda b, t: (b, t, 2)),
                  pl.BlockSpec((1, ts, M_WIDTH), lambda b, t: (b, t, 3)),
                  pl.BlockSpec((1, ts, LANES), lambda b, t: (b, t, 4 * M_WIDTH // LANES)),
                  pl.BlockSpec((M_CONV, 2 * M_WIDTH), lambda b, t: (0, 0)),
                  pl.BlockSpec((1, 2 * M_WIDTH), lambda b, t: (0, 0)),
                  pl.BlockSpec((1, LANES), lambda b, t: (0, 0)),
                  pl.BlockSpec((1, M_WIDTH), lambda b, t: (0, 0))],
        out_specs=pl.BlockSpec((1, ts, M_WIDTH), lambda b, t: (b, t, 0)),
        out_shape=jax.ShapeDtypeStruct((bsz, s, M_WIDTH), F32),
        scratch_shapes=[pltpu.VMEM((M_HEADS, M_HD, M_HD), F32),
                        pltpu.VMEM((M_HEADS, 8, M_HD), F32),
                        pltpu.VMEM((M_HEADS, 8, LANES), F32),
                        pltpu.VMEM((8, 2 * M_WIDTH), F32)],
        compiler_params=_cparams(("parallel", "arbitrary")),
    )(proj_m, proj_m, proj_m, proj_m, cw, cb, gb, ng)


def _rwkv_prep_kernel(p_ref, pp_ref, mu_ref, w0_ref, a0_ref, wlr_ref, g2_ref, kkp_ref, ka_ref,
                      bd_ref, r_o, k_o, v_o, kk_o, a_o, ld_o, g_o):
    ts = p_ref.shape[1]
    p = p_ref[0]
    first = pl.program_id(1) == 0
    prev_last = jnp.where(first, 0.0, pp_ref[0, 7:8, :])
    row = lax.broadcasted_iota(jnp.int32, (ts, 1), 0)
    prev = jnp.where(row == 0, prev_last, pltpu.roll(p, 1, axis=0))
    xs = p + (prev - p) * mu_ref[...]
    rr = xs[:, :R_WIDTH]
    rk = xs[:, R_WIDTH:2 * R_WIDTH]
    rv = xs[:, 2 * R_WIDTH:3 * R_WIDTH]
    gl = xs[:, 3 * R_WIDTH:3 * R_WIDTH + LANES]
    wa = xs[:, 3 * R_WIDTH + LANES:]
    lane = lax.broadcasted_iota(jnp.int32, (1, LANES), 1)
    wa = jnp.where(lane < LANES // 2, jnp.tanh(wa), wa)
    wa_out = _dot(wa.astype(BF16), wlr_ref[...])
    w_log = _log_sigmoid(w0_ref[...] + wa_out[:, :R_WIDTH]) - 0.5
    a = _sigmoid(a0_ref[...] + wa_out[:, R_WIDTH:])
    kk = rk * kkp_ref[...]
    sq = kk * kk
    sq_hi = sq.astype(BF16)
    sq_lo = (sq - sq_hi.astype(F32)).astype(BF16)
    ss = _dot(sq_hi, bd_ref[...]) + _dot(sq_lo, bd_ref[...])
    r_o[0] = rr
    k_o[0] = rk * (1.0 + (a - 1.0) * ka_ref[...])
    v_o[0] = rv
    kk_o[0] = kk / jnp.maximum(jnp.sqrt(ss), 1e-12)
    a_o[0] = a
    ld_o[0] = -jnp.exp(w_log)
    g_o[0] = _dot(_sigmoid(gl).astype(BF16), g2_ref[...])


def _rwkv_prep(proj_r, mu, w0, a0, wlr, g2, kkp, ka, bd, ts=256):
    bsz, s, _ = proj_r.shape
    vec = lambda w: pl.BlockSpec((1, w), lambda b, t: (0, 0))
    full = lambda a: pl.BlockSpec(a.shape, lambda b, t: (0, 0))
    out_spec = pl.BlockSpec((1, ts, R_WIDTH), lambda b, t: (b, t, 0))
    out_sds = jax.ShapeDtypeStruct((bsz, s, R_WIDTH), F32)
    return pl.pallas_call(
        _rwkv_prep_kernel,
        grid=(bsz, s // ts),
        in_specs=[pl.BlockSpec((1, ts, R_COLS), lambda b, t: (b, t, 0)),
                  pl.BlockSpec((1, 8, R_COLS),
                               lambda b, t: (b, jnp.maximum(t * (ts // 8) - 1, 0), 0)),
                  vec(R_COLS), vec(R_WIDTH), vec(R_WIDTH), full(wlr), full(g2),
                  vec(R_WIDTH), vec(R_WIDTH), full(bd)],
        out_specs=[out_spec] * 7,
        out_shape=[out_sds] * 7,
        compiler_params=_cparams(("parallel", "arbitrary")),
    )(proj_r, proj_r, mu, w0, a0, wlr, g2, kkp, ka, bd)


def _rwkv_kernel(r_ref, k_ref, v_ref, kk_ref, a_ref, ld_ref, g_ref, rkp_ref, gng_ref, gnb_ref,
                 out_ref, z_s, *, n_chunks):
    L = CHUNK
    HP = LANES // R_HD

    @pl.when(pl.program_id(1) == 0)
    def _():
        z_s[...] = jnp.zeros_like(z_s)

    row = lax.broadcasted_iota(jnp.int32, (L, L), 0)
    col = lax.broadcasted_iota(jnp.int32, (L, L), 1)
    tril_incl = row >= col
    tril_strict = row > col
    tril_f = tril_incl.astype(F32)
    eye = (row == col).astype(F32)
    lane = lax.broadcasted_iota(jnp.int32, (1, LANES), 1)
    head_masks = [(lane // R_HD) == j for j in range(HP)]
    zr = lax.broadcasted_iota(jnp.int32, (LANES, LANES), 0) // R_HD
    zc = lax.broadcasted_iota(jnp.int32, (LANES, LANES), 1) // R_HD
    same_head = zr == zc

    def per_head(parts):
        out = parts[HP - 1]
        for j in range(HP - 2, -1, -1):
            out = jnp.where(head_masks[j], parts[j], out)
        return out

    def head_sum(x):
        return per_head([jnp.sum(jnp.where(head_masks[j], x, 0.0), axis=1, keepdims=True)
                         for j in range(HP)])

    def chunk(c, carry):
        r0 = pl.multiple_of(c * L, L)
        for p in range(R_WIDTH // LANES):
            sl = slice(p * LANES, (p + 1) * LANES)
            r = r_ref[0, pl.ds(r0, L), sl]
            k = k_ref[0, pl.ds(r0, L), sl]
            v = v_ref[0, pl.ds(r0, L), sl]
            kk = kk_ref[0, pl.ds(r0, L), sl]
            a = a_ref[0, pl.ds(r0, L), sl]
            ld = ld_ref[0, pl.ds(r0, L), sl]
            zt = z_s[p]

            cum = _dot_hi(tril_f, ld)
            dec = jnp.exp(cum)
            inv = jnp.exp(-cum)
            al = -kk * jnp.exp(cum - ld)
            be = kk * a * inv
            kt = k * inv
            rt = r * dec

            ah_parts, uh_parts, intra = [], [], []
            for j in range(HP):
                mj = head_masks[j]
                al_j = jnp.where(mj, al, 0.0)
                rt_j = jnp.where(mj, rt, 0.0)
                ab = jnp.where(tril_strict, _dot_nt_hi(al_j, be), 0.0)
                ak = jnp.where(tril_strict, _dot_nt_hi(al_j, kt), 0.0)
                rb = jnp.where(tril_incl, _dot_nt_hi(rt_j, be), 0.0)
                rkm = jnp.where(tril_incl, _dot_nt_hi(rt_j, kt), 0.0)
                w = eye + ab
                xp = ab
                for _ in range(5):
                    xp = _dot_hi(xp, xp)
                    w = w + _dot_hi(w, xp)
                ah_parts.append(_dot_hi(w, al))
                uh_parts.append(_dot_hi(w, _dot_hi(ak, v)))
                intra.append((rb, rkm))
            ah = per_head(ah_parts)
            uh = per_head(uh_parts)
            u = _dot_nt_hi(ah, zt) + uh
            y = _dot_nt_hi(rt, zt) + per_head(
                [_dot_hi(rb, u) + _dot_hi(rkm, v) for rb, rkm in intra])
            upd = _dot_hi(jnp.concatenate([u, v], axis=0).T, jnp.concatenate([be, kt], axis=0))
            z_s[p] = (zt + jnp.where(same_head, upd, 0.0)) * dec[L - 1:L, :]

            mu = head_sum(y) * (1.0 / R_HD)
            yc = y - mu
            var = head_sum(yc * yc) * (1.0 / R_HD)
            yn = yc * lax.rsqrt(var + RWKV_GN_EPS) * gng_ref[:, sl] + gnb_ref[:, sl]
            bonus = head_sum(r * k * rkp_ref[:, sl]) * v
            out_ref[0, pl.ds(r0, L), sl] = (yn + bonus) * g_ref[0, pl.ds(r0, L), sl]
        return carry

    lax.fori_loop(0, n_chunks, chunk, 0)


def _rwkv(parts, rkp, gng, gnb, ts=256):
    bsz, s, _ = parts[0].shape
    kern = functools.partial(_rwkv_kernel, n_chunks=ts // CHUNK)
    blk = pl.BlockSpec((1, ts, R_WIDTH), lambda b, t: (b, t, 0))
    vec = pl.BlockSpec((1, R_WIDTH), lambda b, t: (0, 0))
    return pl.pallas_call(
        kern,
        grid=(bsz, s // ts),
        in_specs=[blk] * 7 + [vec] * 3,
        out_specs=blk,
        out_shape=jax.ShapeDtypeStruct((bsz, s, R_WIDTH), F32),
        scratch_shapes=[pltpu.VMEM((R_WIDTH // LANES, LANES, LANES), F32)],
        compiler_params=_cparams(("parallel", "arbitrary")),
    )(*parts, rkp, gng, gnb)


def _xattn_kernel(h_ref, kv_ref, wq_ref, wo_ref, g_ref, b_ref, o_ref):
    hq = h_ref[0]
    q = _dot(hq.astype(BF16), wq_ref[...])
    acc = jnp.zeros_like(hq)
    for hd in range(X_HEADS):
        sl = slice(hd * X_HD, (hd + 1) * X_HD)
        kh = kv_ref[0, :, sl]
        vh = kv_ref[0, :, D_MODEL + hd * X_HD:D_MODEL + (hd + 1) * X_HD]
        s = _dot_nt(q[:, sl].astype(BF16), kh) * (X_HD ** -0.5)
        e = jnp.exp(s - jnp.max(s, axis=1, keepdims=True))
        p = e / jnp.sum(e, axis=1, keepdims=True)
        oh = _dot(p.astype(BF16), vh)
        acc = acc + _dot(oh.astype(BF16), wo_ref[sl, :])
    o_ref[0] = _layer_norm(DN_ALPHA * hq + acc, g_ref[...], b_ref[...])


def _xattn(h3, kv3, wq, wo, g, b, tq=512):
    bsz, s, d = h3.shape
    m = kv3.shape[1]
    return pl.pallas_call(
        _xattn_kernel,
        grid=(bsz, s // tq),
        in_specs=[pl.BlockSpec((1, tq, d), lambda i, t: (i, t, 0)),
                  pl.BlockSpec((1, m, 2 * d), lambda i, t: (i, 0, 0)),
                  pl.BlockSpec(wq.shape, lambda i, t: (0, 0)),
                  pl.BlockSpec(wo.shape, lambda i, t: (0, 0)),
                  pl.BlockSpec((1, d), lambda i, t: (0, 0)),
                  pl.BlockSpec((1, d), lambda i, t: (0, 0))],
        out_specs=pl.BlockSpec((1, tq, d), lambda i, t: (i, t, 0)),
        out_shape=jax.ShapeDtypeStruct((bsz, s, d), F32),
        compiler_params=_cparams(("parallel", "parallel")),
    )(h3, kv3, wq, wo, g.reshape(1, d), b.reshape(1, d))


def _router_kernel(h_ref, wr_ref, br_ref, idx_o, gate_o, rank_o, meta_o, blk_o, cnt_s,
                   *, n_blocks):
    tm = h_ref.shape[0]
    step = pl.program_id(0)

    @pl.when(step == 0)
    def _():
        cnt_s[...] = jnp.zeros_like(cnt_s)

    lane = lax.broadcasted_iota(jnp.int32, (1, LANES), 1)
    logits = _dot_hi(h_ref[...], wr_ref[...]) + br_ref[...]
    logits = jnp.where(lane < N_EXPERTS, logits, -jnp.inf)
    vals, sels = [], []
    idx_out = jnp.zeros((tm, LANES), jnp.int32)
    for j in range(TOP_K):
        mx = jnp.max(logits, axis=1, keepdims=True)
        ij = jnp.min(jnp.where(logits == mx, lane, LANES), axis=1, keepdims=True)
        sel = lane == ij
        vals.append(mx)
        sels.append(sel)
        idx_out = jnp.where(lane == j, ij, idx_out)
        logits = jnp.where(sel, -jnp.inf, logits)
    exps = [jnp.exp(vj - vals[0]) for vj in vals]
    denom = exps[0] + exps[1] + exps[2] + exps[3]
    gate_out = jnp.zeros((tm, LANES), F32)
    for j in range(TOP_K):
        gate_out = jnp.where(lane == j, exps[j] / denom, gate_out)

    cnt = jnp.zeros((tm, LANES), F32)
    for sel in sels:
        cnt = cnt + sel.astype(F32)
    row = lax.broadcasted_iota(jnp.int32, (tm, tm), 0)
    col = lax.broadcasted_iota(jnp.int32, (tm, tm), 1)
    before = (row > col).astype(BF16)
    excl = _dot(before, cnt.astype(BF16)) + cnt_s[0:1, :]
    rank_out = jnp.zeros((tm, LANES), jnp.int32)
    for j in range(TOP_K):
        rj = jnp.sum(jnp.where(sels[j], excl, 0.0), axis=1, keepdims=True)
        rank_out = jnp.where(lane == j, rj.astype(jnp.int32), rank_out)
    total = cnt_s[0:1, :] + jnp.sum(cnt, axis=0, keepdims=True)
    cnt_s[...] = jnp.broadcast_to(total, cnt_s.shape)

    idx_o[...] = idx_out
    gate_o[...] = gate_out
    rank_o[...] = rank_out

    padded = jnp.floor((total + (MOE_ROWS - 1)) * (1.0 / MOE_ROWS)) * MOE_ROWS
    er = lax.broadcasted_iota(jnp.int32, (LANES, LANES), 0)
    ec = lax.broadcasted_iota(jnp.int32, (LANES, LANES), 1)
    pad_end = _dot_hi(padded, (er <= ec).astype(F32))
    pad_start = pad_end - padded
    meta = jnp.where(lax.broadcasted_iota(jnp.int32, (8, LANES), 0) == 0, pad_start, pad_end)
    meta_o[...] = meta.astype(jnp.int32)
    brow = lax.broadcasted_iota(jnp.int32, (n_blocks, LANES), 0).astype(F32) * MOE_ROWS
    done = jnp.where((lane < N_EXPERTS) & (pad_end <= brow), 1.0, 0.0)
    be = jnp.minimum(jnp.sum(done, axis=1, keepdims=True), N_EXPERTS - 1.0)
    blk_o[...] = jnp.broadcast_to(be, (n_blocks, LANES)).astype(jnp.int32)


def _router(h, wr, br, n_blocks, tm=512):
    n, d = h.shape
    kern = functools.partial(_router_kernel, n_blocks=n_blocks)
    tok = pl.BlockSpec((tm, LANES), lambda i: (i, 0))
    return pl.pallas_call(
        kern,
        grid=(n // tm,),
        in_specs=[pl.BlockSpec((tm, d), lambda i: (i, 0)),
                  pl.BlockSpec(wr.shape, lambda i: (0, 0)),
                  pl.BlockSpec((1, LANES), lambda i: (0, 0))],
        out_specs=[tok, tok, tok,
                   pl.BlockSpec((8, LANES), lambda i: (0, 0)),
                   pl.BlockSpec((n_blocks, LANES), lambda i: (0, 0))],
        out_shape=[jax.ShapeDtypeStruct((n, LANES), jnp.int32),
                   jax.ShapeDtypeStruct((n, LANES), F32),
                   jax.ShapeDtypeStruct((n, LANES), jnp.int32),
                   jax.ShapeDtypeStruct((8, LANES), jnp.int32),
                   jax.ShapeDtypeStruct((n_blocks, LANES), jnp.int32)],
        scratch_shapes=[pltpu.VMEM((8, LANES), F32)],
        compiler_params=_cparams(("arbitrary",)),
    )(h, wr, br)


def _dispatch_kernel(idx_ref, rank_ref, start_ref, x_ref, buf_in, buf_ref, sem):
    del buf_in
    tm = x_ref.shape[0]

    def copy(i):
        dest = start_ref[idx_ref[i]] + rank_ref[i]
        return pltpu.make_async_copy(x_ref.at[pl.ds(i // TOP_K, 1), :],
                                     buf_ref.at[pl.ds(dest, 1), :], sem)

    def issue(i, c):
        copy(i).start()
        return c

    def drain(i, c):
        copy(i).wait()
        return c

    lax.fori_loop(0, tm * TOP_K, issue, 0)
    lax.fori_loop(0, tm * TOP_K, drain, 0)


def _dispatch(h, idx_flat, rank_flat, pad_start, n_rows, tm=256):
    n, d = h.shape
    flat = pl.BlockSpec((tm * TOP_K,), lambda i: (i,), memory_space=pltpu.SMEM)
    return pl.pallas_call(
        _dispatch_kernel,
        grid=(n // tm,),
        in_specs=[flat, flat,
                  pl.BlockSpec(memory_space=pltpu.SMEM),
                  pl.BlockSpec((tm, d), lambda i: (i, 0)),
                  pl.BlockSpec(memory_space=pl.ANY)],
        out_specs=pl.BlockSpec(memory_space=pl.ANY),
        out_shape=jax.ShapeDtypeStruct((n_rows, d), F32),
        scratch_shapes=[pltpu.SemaphoreType.DMA(())],
        input_output_aliases={4: 0},
        compiler_params=_cparams(("arbitrary",)),
    )(idx_flat, rank_flat, pad_start, h, jnp.zeros((n_rows, d), F32))


def _expert_kernel(be_ref, nb_ref, x_ref, w1_ref, b1_ref, w2_ref, b2_ref, y_ref):
    del be_ref
    used = pl.program_id(0) < nb_ref[0]

    @pl.when(jnp.logical_not(used))
    def _():
        y_ref[...] = jnp.zeros_like(y_ref)

    @pl.when(used)
    def _():
        hu = _dot(x_ref[...].astype(BF16), w1_ref[0]) + b1_ref[0]
        gate = jnp.minimum(hu[:, :D_FF], SWIGLU_LIMIT)
        up = jnp.clip(hu[:, D_FF:], -SWIGLU_LIMIT, SWIGLU_LIMIT)
        glu = gate * _sigmoid(gate * SWIGLU_ALPHA)
        y_ref[...] = _dot(((up + 1.0) * glu).astype(BF16), w2_ref[0]) + b2_ref[0]


def _experts(buf, block_e, n_used, w1, b1, w2, b2):
    n_rows, d = buf.shape
    n_blocks = n_rows // MOE_ROWS
    rows = lambda i, be, nb: (jnp.minimum(i, nb[0] - 1), 0)
    wsel = lambda i, be, nb: (be[jnp.minimum(i, nb[0] - 1)], 0, 0)
    return pl.pallas_call(
        _expert_kernel,
        grid_spec=pltpu.PrefetchScalarGridSpec(
            num_scalar_prefetch=2,
            grid=(n_blocks,),
            in_specs=[pl.BlockSpec((MOE_ROWS, d), rows),
                      pl.BlockSpec((1, d, 2 * D_FF), wsel),
                      pl.BlockSpec((1, 1, 2 * D_FF), wsel),
                      pl.BlockSpec((1, D_FF, d), wsel),
                      pl.BlockSpec((1, 1, d), wsel)],
            out_specs=pl.BlockSpec((MOE_ROWS, d), lambda i, be, nb: (i, 0))),
        out_shape=jax.ShapeDtypeStruct((n_rows, d), F32),
        compiler_params=_cparams(("arbitrary",)),
    )(block_e, n_used, buf, w1, b1, w2, b2)


def _combine_kernel(idx_ref, rank_ref, start_ref, gate_ref, h_ref, g_ref, b_ref, y_hbm, o_ref,
                    gbuf, sem):
    tm = h_ref.shape[0]

    def copy(i):
        src = start_ref[idx_ref[i]] + rank_ref[i]
        return pltpu.make_async_copy(y_hbm.at[pl.ds(src, 1), :],
                                     gbuf.at[i % TOP_K, pl.ds(i // TOP_K, 1), :], sem)

    def issue(i, c):
        copy(i).start()
        return c

    def drain(i, c):
        copy(i).wait()
        return c

    lax.fori_loop(0, tm * TOP_K, issue, 0)
    lax.fori_loop(0, tm * TOP_K, drain, 0)
    moe = gate_ref[:, 0:1] * gbuf[0]
    for j in range(1, TOP_K):
        moe = moe + gate_ref[:, j:j + 1] * gbuf[j]
    o_ref[...] = _layer_norm(DN_ALPHA * h_ref[...] + moe, g_ref[...], b_ref[...])


def _combine(h, y_buf, idx_flat, rank_flat, pad_start, gates, g, b, tm=256):
    n, d = h.shape
    flat = pl.BlockSpec((tm * TOP_K,), lambda i: (i,), memory_space=pltpu.SMEM)
    return pl.pallas_call(
        _combine_kernel,
        grid=(n // tm,),
        in_specs=[flat, flat,
                  pl.BlockSpec(memory_space=pltpu.SMEM),
                  pl.BlockSpec((tm, LANES), lambda i: (i, 0)),
                  pl.BlockSpec((tm, d), lambda i: (i, 0)),
                  pl.BlockSpec((1, d), lambda i: (0, 0)),
                  pl.BlockSpec((1, d), lambda i: (0, 0)),
                  pl.BlockSpec(memory_space=pl.ANY)],
        out_specs=pl.BlockSpec((tm, d), lambda i: (i, 0)),
        out_shape=jax.ShapeDtypeStruct((n, d), F32),
        scratch_shapes=[pltpu.VMEM((TOP_K, tm, d), F32), pltpu.SemaphoreType.DMA(())],
        compiler_params=_cparams(("arbitrary",)),
    )(idx_flat, rank_flat, pad_start, gates, h, g.reshape(1, d), b.reshape(1, d), y_buf)


def _moe(h, wr, br, w1, b1, w2, b2, g, b):
    n, d = h.shape
    n_blocks = n * TOP_K // MOE_ROWS + N_EXPERTS
    wr_pad = jnp.pad(wr, ((0, 0), (0, LANES - N_EXPERTS)))
    br_pad = jnp.pad(br, (0, LANES - N_EXPERTS)).reshape(1, LANES)
    idx, gates, rank, meta, blk = _router(h, wr_pad, br_pad, n_blocks)
    idx_flat = idx[:, :TOP_K].reshape(-1)
    rank_flat = rank[:, :TOP_K].reshape(-1)
    pad_start = meta[0, :N_EXPERTS]
    n_used = meta[1, N_EXPERTS - 1:N_EXPERTS] // MOE_ROWS
    buf = _dispatch(h, idx_flat, rank_flat, pad_start, n_blocks * MOE_ROWS)
    y_buf = _experts(buf, blk[:, 0], n_used, w1.astype(BF16), b1[:, None, :],
                     w2.astype(BF16), b2[:, None, :])
    return _combine(h, y_buf, idx_flat, rank_flat, pad_start, gates, g, b)


def _block_diag_ones(width, head):
    r = jnp.arange(width) // head
    return (r[:, None] == r[None, :]).astype(BF16)


def _mixer_layer(h, bsz, s, p, l, debug=False):
    m_cols = 4 * M_WIDTH + 2 * M_HEADS
    rw = 3 * R_WIDTH
    row = lambda a: a.reshape(1, -1)
    n = bsz * s
    wl = p['w_in'][l]
    w_m = jnp.pad(wl[:, :m_cols], ((0, 0), (0, M_COLS_PAD - m_cols))).astype(BF16)
    perm = lambda a: jnp.concatenate([a[..., :rw], a[..., rw + 128:], a[..., rw:rw + 128]], axis=-1)
    w_r = perm(wl[:, m_cols:]).astype(BF16)
    proj_m, proj_r = _proj(h, w_m, w_r)

    gate_bias = jnp.pad(jnp.concatenate([p['m_ig_b'][l], p['m_fg_b'][l]]),
                        (0, LANES - 2 * M_HEADS))
    hm = _mlstm(proj_m.reshape(bsz, s, M_COLS_PAD), p['m_conv_w'][l], row(p['m_conv_b'][l]),
                row(gate_bias), row(p['m_norm_g'][l]))

    zeros = jnp.zeros((R_WIDTH // 8, R_WIDTH), F32)
    w_lr = jnp.concatenate([jnp.concatenate([p['r_w2'][l], zeros], axis=1),
                            jnp.concatenate([zeros, p['r_a2'][l]], axis=1)], axis=0).astype(BF16)
    parts = _rwkv_prep(proj_r.reshape(bsz, s, R_COLS), row(perm(p['r_mu'][l])), row(p['r_w0'][l]),
                       row(p['r_a0'][l]), w_lr, p['r_g2'][l].astype(BF16), row(p['r_kk'][l]),
                       row(p['r_ka'][l]), _block_diag_ones(R_WIDTH, R_HD))
    hr = _rwkv(parts, row(p['r_rk'][l]), row(p['r_gn_g'][l]), row(p['r_gn_b'][l]))

    h1 = _mix_out(hm.reshape(n, M_WIDTH), hr.reshape(n, R_WIDTH), h, p['w_out'][l].astype(BF16),
                  p['ln1_g'][l], p['ln1_b'][l])
    if debug:
        return h1, hm, hr
    return h1


def _xattn_layer(h, mem2, bsz, s, p, l):
    n, d = h.shape
    n_mem = mem2.shape[0] // bsz
    kv = _mm(mem2, p['x_wkv'][l].astype(BF16), BF16)
    return _xattn(h.reshape(bsz, s, d), kv.reshape(bsz, n_mem, 2 * d), p['x_wq'][l].astype(BF16),
                  p['x_wo'][l].astype(BF16), p['ln2_g'][l], p['ln2_b'][l]).reshape(n, d)


def kernel(x, mem, ln0_g, ln0_b, w_in, m_conv_w, m_conv_b, m_ig_b, m_fg_b, m_norm_g, r_mu, r_w0,
           r_w2, r_a0, r_a2, r_g2, r_kk, r_ka, r_rk, r_gn_g, r_gn_b, w_out, ln1_g, ln1_b, x_wq,
           x_wkv, x_wo, ln2_g, ln2_b, moe_wr, moe_br, moe_w1, moe_b1, moe_w2, moe_b2, ln3_g,
           ln3_b):
    p = dict(w_in=w_in, m_conv_w=m_conv_w, m_conv_b=m_conv_b, m_ig_b=m_ig_b, m_fg_b=m_fg_b,
             m_norm_g=m_norm_g, r_mu=r_mu, r_w0=r_w0, r_w2=r_w2, r_a0=r_a0, r_a2=r_a2, r_g2=r_g2,
             r_kk=r_kk, r_ka=r_ka, r_rk=r_rk, r_gn_g=r_gn_g, r_gn_b=r_gn_b, w_out=w_out,
             ln1_g=ln1_g, ln1_b=ln1_b, x_wq=x_wq, x_wkv=x_wkv, x_wo=x_wo, ln2_g=ln2_g,
             ln2_b=ln2_b)
    bsz, s, d = x.shape
    n = bsz * s
    h = _ln(x.reshape(n, d), ln0_g, ln0_b)
    mem2 = mem.reshape(bsz * mem.shape[1], d)
    for l in range(DEPTH):
        h = _mixer_layer(h, bsz, s, p, l)
        h = _xattn_layer(h, mem2, bsz, s, p, l)
        h = _moe(h, moe_wr[l], moe_br[l], moe_w1[l], moe_b1[l], moe_w2[l], moe_b2[l],
                 ln3_g[l], ln3_b[l])
    return h.reshape(bsz, s, d)
```

```python
import functools

import jax
import jax.numpy as jnp
from jax import lax
from jax.experimental import pallas as pl
from jax.experimental.pallas import tpu as pltpu

F32 = jnp.float32
BF16 = jnp.bfloat16
HIGHEST = lax.Precision.HIGHEST

D_MODEL = 1024
DEPTH = 2
CHUNK = 64
M_WIDTH = 512
M_HEADS = 4
M_HD = 128
M_CONV = 4
R_WIDTH = 512
R_HD = 64
R_HEADS = 8
R_LR = 256
X_HEADS = 4
X_HD = 256
N_EXPERTS = 32
TOP_K = 4
D_FF = 1024
SWIGLU_LIMIT = 7.0
SWIGLU_ALPHA = 1.702
LN_EPS = 1e-5
HEAD_NORM_EPS = 1e-5
RWKV_GN_EPS = 64e-5
DN_ALPHA = (2 * DEPTH) ** 0.25

LANES = 128
M_COLS_PAD = 4 * M_WIDTH + LANES
R_COLS = 3 * R_WIDTH + R_LR
R_GROUP = 256
DMA_UNROLL = 8
MOE_ROWS = 256
VMEM_LIMIT = 56 * 1024 * 1024


def _cparams(sem, vmem=VMEM_LIMIT):
    return pltpu.CompilerParams(dimension_semantics=sem, vmem_limit_bytes=vmem)


def _dot(a, b):
    return jnp.dot(a, b, preferred_element_type=F32)


def _dot_nt(a, b):
    return lax.dot_general(a, b, (((1,), (1,)), ((), ())), preferred_element_type=F32)


def _dot_hi(a, b):
    return jnp.dot(a, b, preferred_element_type=F32, precision=HIGHEST)


def _dot_nt_hi(a, b):
    return lax.dot_general(a, b, (((1,), (1,)), ((), ())), preferred_element_type=F32,
                           precision=HIGHEST)


def _layer_norm(x, g, b):
    mu = jnp.mean(x, axis=-1, keepdims=True)
    xc = x - mu
    var = jnp.mean(xc * xc, axis=-1, keepdims=True)
    return xc * lax.rsqrt(var + LN_EPS) * g + b


def _sigmoid(x):
    return 1.0 / (1.0 + jnp.exp(-x))


def _log_sigmoid(x):
    return jnp.minimum(x, 0.0) - jnp.log1p(jnp.exp(-jnp.abs(x)))


def _ln_kernel(x_ref, g_ref, b_ref, o_ref):
    o_ref[...] = _layer_norm(x_ref[...], g_ref[...], b_ref[...])


def _ln(x, g, b, tm=512):
    n, d = x.shape
    return pl.pallas_call(
        _ln_kernel,
        grid=(n // tm,),
        in_specs=[pl.BlockSpec((tm, d), lambda i: (i, 0)),
                  pl.BlockSpec((1, d), lambda i: (0, 0)),
                  pl.BlockSpec((1, d), lambda i: (0, 0))],
        out_specs=pl.BlockSpec((tm, d), lambda i: (i, 0)),
        out_shape=jax.ShapeDtypeStruct((n, d), F32),
        compiler_params=_cparams(("parallel",)),
    )(x, g.reshape(1, d), b.reshape(1, d))


def _proj_kernel(x_ref, wm_ref, wr_ref, om_ref, or_ref):
    x = x_ref[...].astype(BF16)
    om_ref[...] = _dot(x, wm_ref[...])
    or_ref[...] = _dot(x, wr_ref[...])


def _proj(h, wm, wr, tm=512):
    n, d = h.shape
    return pl.pallas_call(
        _proj_kernel,
        grid=(n // tm,),
        in_specs=[pl.BlockSpec((tm, d), lambda i: (i, 0)),
                  pl.BlockSpec(wm.shape, lambda i: (0, 0)),
                  pl.BlockSpec(wr.shape, lambda i: (0, 0))],
        out_specs=[pl.BlockSpec((tm, wm.shape[1]), lambda i: (i, 0)),
                   pl.BlockSpec((tm, wr.shape[1]), lambda i: (i, 0))],
        out_shape=[jax.ShapeDtypeStruct((n, wm.shape[1]), F32),
                   jax.ShapeDtypeStruct((n, wr.shape[1]), F32)],
        compiler_params=_cparams(("parallel",)),
    )(h, wm, wr)


def _mm_kernel(x_ref, w_ref, o_ref):
    o_ref[...] = _dot(x_ref[...].astype(BF16), w_ref[...]).astype(o_ref.dtype)


def _mm(x, w, out_dtype, tm=512):
    n, d = x.shape
    return pl.pallas_call(
        _mm_kernel,
        grid=(n // tm,),
        in_specs=[pl.BlockSpec((tm, d), lambda i: (i, 0)),
                  pl.BlockSpec(w.shape, lambda i: (0, 0))],
        out_specs=pl.BlockSpec((tm, w.shape[1]), lambda i: (i, 0)),
        out_shape=jax.ShapeDtypeStruct((n, w.shape[1]), out_dtype),
        compiler_params=_cparams(("parallel",)),
    )(x, w)


def _mix_out_kernel(hm_ref, hr_ref, h_ref, w_ref, g_ref, b_ref, o_ref):
    mix = _dot(hm_ref[...].astype(BF16), w_ref[:M_WIDTH, :])
    mix = mix + _dot(hr_ref[...].astype(BF16), w_ref[M_WIDTH:, :])
    o_ref[...] = _layer_norm(DN_ALPHA * h_ref[...] + mix, g_ref[...], b_ref[...])


def _mix_out(hm, hr, h, w, g, b, tm=512):
    n, d = h.shape
    return pl.pallas_call(
        _mix_out_kernel,
        grid=(n // tm,),
        in_specs=[pl.BlockSpec((tm, M_WIDTH), lambda i: (i, 0)),
                  pl.BlockSpec((tm, R_WIDTH), lambda i: (i, 0)),
                  pl.BlockSpec((tm, d), lambda i: (i, 0)),
                  pl.BlockSpec(w.shape, lambda i: (0, 0)),
                  pl.BlockSpec((1, d), lambda i: (0, 0)),
                  pl.BlockSpec((1, d), lambda i: (0, 0))],
        out_specs=pl.BlockSpec((tm, d), lambda i: (i, 0)),
        out_shape=jax.ShapeDtypeStruct((n, d), F32),
        compiler_params=_cparams(("parallel",)),
    )(hm, hr, h, w, g.reshape(1, d), b.reshape(1, d))


def _mlstm_kernel(qk_ref, v_ref, o_ref, gt_ref, cw_ref, cb_ref, gb_ref, ng_ref, out_ref,
                  ct_s, n_s, m_s, tail_s, *, n_chunks):
    L = CHUNK

    @pl.when(pl.program_id(1) == 0)
    def _():
        ct_s[...] = jnp.zeros_like(ct_s)
        n_s[...] = jnp.zeros_like(n_s)
        m_s[...] = jnp.zeros_like(m_s)
        tail_s[...] = jnp.zeros_like(tail_s)

    row = lax.broadcasted_iota(jnp.int32, (L, L), 0)
    col = lax.broadcasted_iota(jnp.int32, (L, L), 1)
    causal = row >= col
    tril_f = causal.astype(F32)
    lane = lax.broadcasted_iota(jnp.int32, (1, LANES), 1)

    def chunk(c, carry):
        r0 = pl.multiple_of(c * L, L)
        x = qk_ref[0, pl.ds(r0, L), :]
        cat = jnp.concatenate([tail_s[...], x], axis=0)
        acc = cb_ref[...] + cw_ref[M_CONV - 1:M_CONV, :] * x
        for j in range(M_CONV - 1):
            acc = acc + cw_ref[j:j + 1, :] * pltpu.roll(cat, M_CONV - 1 - j, axis=0)[8:, :]
        tail_s[...] = x[L - 8:, :]
        qk = acc * _sigmoid(acc)

        gx = gt_ref[0, pl.ds(r0, L), :] + gb_ref[...]
        bc = _dot_hi(tril_f, _log_sigmoid(gx))
        rows = jnp.where(lane < M_HEADS, gx, bc).T

        for h in range(M_HEADS):
            q = qk[:, h * M_HD:(h + 1) * M_HD]
            k = qk[:, M_WIDTH + h * M_HD:M_WIDTH + (h + 1) * M_HD] * (M_HD ** -0.5)
            v = v_ref[0, pl.ds(r0, L), h * M_HD:(h + 1) * M_HD]
            ig_c = gx[:, h:h + 1]
            b_c = bc[:, M_HEADS + h:M_HEADS + h + 1]
            ig_r = rows[h:h + 1, :]
            b_r = rows[M_HEADS + h:M_HEADS + h + 1, :]
            b_last = bc[L - 1:L, M_HEADS + h:M_HEADS + h + 1]
            m_prev = m_s[h, 0:1, 0:1]
            n_prev = n_s[h, 0:1, :]
            ct_prev = ct_s[h]

            qb = q.astype(BF16)
            kb = k.astype(BF16)
            vb = v.astype(BF16)

            g_inter = b_c + m_prev
            d_mat = jnp.where(causal, b_c - b_r + ig_r, -jnp.inf)
            m_t = jnp.maximum(g_inter, jnp.max(d_mat, axis=1, keepdims=True))
            s = _dot_nt(qb, kb) * jnp.exp(d_mat - m_t)
            w_inter = jnp.exp(g_inter - m_t)
            num = _dot(s.astype(BF16), vb) + w_inter * _dot(qb, ct_prev.astype(BF16))
            den = (jnp.sum(s, axis=1, keepdims=True)
                   + w_inter * jnp.sum(q * n_prev, axis=1, keepdims=True))
            hc = num / jnp.maximum(jnp.abs(den), jnp.exp(-m_t))
            mu = jnp.mean(hc, axis=1, keepdims=True)
            hcc = hc - mu
            var = jnp.mean(hcc * hcc, axis=1, keepdims=True)
            og = o_ref[0, pl.ds(r0, L), h * M_HD:(h + 1) * M_HD]
            out = hcc * lax.rsqrt(var + HEAD_NORM_EPS) * ng_ref[:, h * M_HD:(h + 1) * M_HD]
            out_ref[0, pl.ds(r0, L), h * M_HD:(h + 1) * M_HD] = out * _sigmoid(og)

            src = b_last - b_c + ig_c
            m_loc = jnp.max(src, axis=0, keepdims=True)
            kw = k * jnp.exp(src - m_loc)
            ct_loc = _dot(kw.T.astype(BF16), vb)
            n_loc = jnp.sum(kw, axis=0, keepdims=True)
            m_new = jnp.maximum(b_last + m_prev, m_loc)
            a = jnp.exp(b_last + m_prev - m_new)
            cc = jnp.exp(m_loc - m_new)
            ct_s[h] = a * ct_prev + cc * ct_loc
            n_s[h] = jnp.broadcast_to(a * n_prev + cc * n_loc, (8, M_HD))
            m_s[h] = jnp.broadcast_to(m_new, (8, LANES))
        return carry

    lax.fori_loop(0, n_chunks, chunk, 0)


def _mlstm(proj_m, cw, cb, gb, ng, ts=512):
    bsz, s, _ = proj_m.shape
    kern = functools.partial(_mlstm_kernel, n_chunks=ts // CHUNK)
    return pl.pallas_call(
        kern,
        grid=(bsz, s // ts),
        in_specs=[pl.BlockSpec((1, ts, 2 * M_WIDTH), lambda b, t: (b, t, 0)),
                  pl.BlockSpec((1, ts, M_WIDTH), lambda b, t: (b, t, 2)),
                  pl.BlockSpec((1, ts, M_WIDTH), lambda b, t: (b, t, 3)),
                  pl.BlockSpec((1, ts, LANES), lambda b, t: (b, t, 4 * M_WIDTH // LANES)),
                  pl.BlockSpec((M_CONV, 2 * M_WIDTH), lambda b, t: (0, 0)),
                  pl.BlockSpec((1, 2 * M_WIDTH), lambda b, t: (0, 0)),
                  pl.BlockSpec((1, LANES), lambda b, t: (0, 0)),
                  pl.BlockSpec((1, M_WIDTH), lambda b, t: (0, 0))],
        out_specs=pl.BlockSpec((1, ts, M_WIDTH), lambda b, t: (b, t, 0)),
        out_shape=jax.ShapeDtypeStruct((bsz, s, M_WIDTH), F32),
        scratch_shapes=[pltpu.VMEM((M_HEADS, M_HD, M_HD), F32),
                        pltpu.VMEM((M_HEADS, 8, M_HD), F32),
                        pltpu.VMEM((M_HEADS, 8, LANES), F32),
                        pltpu.VMEM((8, 2 * M_WIDTH), F32)],
        compiler_params=_cparams(("parallel", "arbitrary")),
    )(proj_m, proj_m, proj_m, proj_m, cw, cb, gb, ng)


def _rwkv_prep_kernel(p_ref, pp_ref, mu_ref, w0_ref, a0_ref, wlr_ref, g2_ref, kkp_ref, ka_ref,
                      bd_ref, r_o, k_o, v_o, kk_o, a_o, ld_o, g_o):
    ts = p_ref.shape[1]
    p = p_ref[0]
    first = pl.program_id(1) == 0
    prev_last = jnp.where(first, 0.0, pp_ref[0, 7:8, :])
    row = lax.broadcasted_iota(jnp.int32, (ts, 1), 0)
    prev = jnp.where(row == 0, prev_last, pltpu.roll(p, 1, axis=0))
    xs = p + (prev - p) * mu_ref[...]
    rr = xs[:, :R_WIDTH]
    rk = xs[:, R_WIDTH:2 * R_WIDTH]
    rv = xs[:, 2 * R_WIDTH:3 * R_WIDTH]
    gl = xs[:, 3 * R_WIDTH:3 * R_WIDTH + LANES]
    wa = xs[:, 3 * R_WIDTH + LANES:]
    lane = lax.broadcasted_iota(jnp.int32, (1, LANES), 1)
    wa = jnp.where(lane < LANES // 2, jnp.tanh(wa), wa)
    wa_out = _dot(wa.astype(BF16), wlr_ref[...])
    w_log = _log_sigmoid(w0_ref[...] + wa_out[:, :R_WIDTH]) - 0.5
    a = _sigmoid(a0_ref[...] + wa_out[:, R_WIDTH:])
    kk = rk * kkp_ref[...]
    sq = kk * kk
    sq_hi = sq.astype(BF16)
    sq_lo = (sq - sq_hi.astype(F32)).astype(BF16)
    ss = _dot(sq_hi, bd_ref[...]) + _dot(sq_lo, bd_ref[...])
    r_o[0] = rr
    k_o[0] = rk * (1.0 + (a - 1.0) * ka_ref[...])
    v_o[0] = rv
    kk_o[0] = kk / jnp.maximum(jnp.sqrt(ss), 1e-12)
    a_o[0] = a
    ld_o[0] = -jnp.exp(w_log)
    g_o[0] = _dot(_sigmoid(gl).astype(BF16), g2_ref[...])


def _rwkv_prep(proj_r, mu, w0, a0, wlr, g2, kkp, ka, bd, ts=256):
    bsz, s, _ = proj_r.shape
    vec = lambda w: pl.BlockSpec((1, w), lambda b, t: (0, 0))
    full = lambda a: pl.BlockSpec(a.shape, lambda b, t: (0, 0))
    out_spec = pl.BlockSpec((1, ts, R_WIDTH), lambda b, t: (b, t, 0))
    out_sds = jax.ShapeDtypeStruct((bsz, s, R_WIDTH), F32)
    return pl.pallas_call(
        _rwkv_prep_kernel,
        grid=(bsz, s // ts),
        in_specs=[pl.BlockSpec((1, ts, R_COLS), lambda b, t: (b, t, 0)),
                  pl.BlockSpec((1, 8, R_COLS),
                               lambda b, t: (b, jnp.maximum(t * (ts // 8) - 1, 0), 0)),
                  vec(R_COLS), vec(R_WIDTH), vec(R_WIDTH), full(wlr), full(g2),
                  vec(R_WIDTH), vec(R_WIDTH), full(bd)],
        out_specs=[out_spec] * 7,
        out_shape=[out_sds] * 7,
        compiler_params=_cparams(("parallel", "arbitrary")),
    )(proj_r, proj_r, mu, w0, a0, wlr, g2, kkp, ka, bd)


def _group_head_sum(x, masks):
    out = jnp.sum(jnp.where(masks[-1], x, 0.0), axis=1, keepdims=True)
    for m in masks[-2::-1]:
        out = jnp.where(m, jnp.sum(jnp.where(m, x, 0.0), axis=1, keepdims=True), out)
    return out


def _rwkv_intra_kernel(r_ref, k_ref, v_ref, kk_ref, a_ref, ld_ref, g_ref, rkp_ref, gng_ref,
                       gnb_ref, ah_o, rq_o, be_o, kt_o, uh_o, y0_o, e1_o, e2_o, dl_o, *, n_chunks):
    L = CHUNK
    G = R_GROUP
    row = lax.broadcasted_iota(jnp.int32, (L, L), 0)
    col = lax.broadcasted_iota(jnp.int32, (L, L), 1)
    tril_b = (row >= col).astype(BF16)
    br = lax.broadcasted_iota(jnp.int32, (G, G), 0)
    bc = lax.broadcasted_iota(jnp.int32, (G, G), 1)
    same = (br // L) == (bc // L)
    lower_s = same & ((br % L) > (bc % L))
    lower_i = same & ((br % L) >= (bc % L))
    eye = (br == bc).astype(F32)
    lane = lax.broadcasted_iota(jnp.int32, (1, G), 1)
    masks = [(lane // R_HD) == h for h in range(G // R_HD)]

    def stack(x):
        return jnp.concatenate([jnp.where(m, x, 0.0) for m in masks], axis=0).astype(BF16)

    def collapse(x):
        out = x[0:L]
        for h in range(1, G // R_HD):
            out = out + x[h * L:(h + 1) * L]
        return out

    for ci in range(n_chunks):
        rows = slice(ci * L, (ci + 1) * L)
        for gi in range(R_WIDTH // G):
            sl = slice(gi * G, (gi + 1) * G)
            r = r_ref[0, rows, sl]
            k = k_ref[0, rows, sl]
            v = v_ref[0, rows, sl]
            kk = kk_ref[0, rows, sl]
            a = a_ref[0, rows, sl]
            ld = ld_ref[0, rows, sl]

            ld_hi = ld.astype(BF16)
            ld_lo = (ld - ld_hi.astype(F32)).astype(BF16)
            cum = _dot(tril_b, ld_hi) + _dot(tril_b, ld_lo)
            dec = jnp.exp(cum)
            inv = jnp.exp(-cum)
            al = -kk * jnp.exp(cum - ld)
            be = kk * a * inv
            kt = k * inv
            rt = r * dec
            dl = dec[L - 1:L, :]

            al4 = stack(al)
            v4 = stack(v)
            big = _dot_nt(jnp.concatenate([al4, stack(rt)], axis=0),
                          jnp.concatenate([stack(be), stack(kt)], axis=0))
            ab = jnp.where(lower_s, big[:G, :G], 0.0)
            ak = jnp.where(lower_s, big[:G, G:], 0.0)
            rb = jnp.where(lower_i, big[G:, :G], 0.0).astype(BF16)
            rk = jnp.where(lower_i, big[G:, G:], 0.0).astype(BF16)
            w = eye + ab
            xp = ab
            for _ in range(5):
                xpb = xp.astype(BF16)
                xp = _dot(xpb, xpb)
                w = w + _dot(w.astype(BF16), xp.astype(BF16))
            wb = w.astype(BF16)
            ahs = _dot(wb, al4)
            uhs = _dot(wb, _dot(ak.astype(BF16), v4).astype(BF16))
            rqs = _dot(rb, ahs.astype(BF16))
            y0s = _dot(rb, uhs.astype(BF16)) + _dot(rk, v4)

            ah_o[0, rows, sl] = collapse(ahs).astype(BF16)
            rq_o[0, rows, sl] = (rt + collapse(rqs)).astype(BF16)
            be_o[0, rows, sl] = (be * dl).astype(BF16)
            kt_o[0, rows, sl] = (kt * dl).astype(BF16)
            uh_o[0, rows, sl] = collapse(uhs)
            y0_o[0, rows, sl] = collapse(y0s)
            g = g_ref[0, rows, sl]
            bonus = _group_head_sum(r * k * rkp_ref[:, sl], masks) * v
            e1_o[0, rows, sl] = g * gng_ref[:, sl]
            e2_o[0, rows, sl] = (gnb_ref[:, sl] + bonus) * g
            dl_o[0, ci * 8:(ci + 1) * 8, sl] = jnp.broadcast_to(dl, (8, G))


def _rwkv_seq_kernel(ah_ref, rq_ref, be_ref, kt_ref, v_ref, uh_ref, y0_ref, e1_ref, e2_ref, dl_ref,
                     out_ref, z_s, *, n_chunks):
    L = CHUNK
    G = R_GROUP

    @pl.when(pl.program_id(1) == 0)
    def _():
        z_s[...] = jnp.zeros_like(z_s)

    zr = lax.broadcasted_iota(jnp.int32, (G, G), 0) // R_HD
    zc = lax.broadcasted_iota(jnp.int32, (G, G), 1) // R_HD
    same_head = zr == zc
    lane = lax.broadcasted_iota(jnp.int32, (1, G), 1)
    masks = [(lane // R_HD) == h for h in range(G // R_HD)]

    def chunk(c, carry):
        r0 = pl.multiple_of(c * L, L)
        d0 = pl.multiple_of(c * 8, 8)
        for gi in range(R_WIDTH // G):
            sl = slice(gi * G, (gi + 1) * G)
            zt = z_s[gi]
            lhs = jnp.concatenate([ah_ref[0, pl.ds(r0, L), sl], rq_ref[0, pl.ds(r0, L), sl]],
                                  axis=0)
            pr = _dot_nt(lhs, zt.astype(BF16))
            u = pr[:L] + uh_ref[0, pl.ds(r0, L), sl]
            y = pr[L:] + y0_ref[0, pl.ds(r0, L), sl]
            uv = jnp.concatenate([u, v_ref[0, pl.ds(r0, L), sl]], axis=0)
            bk = jnp.concatenate([be_ref[0, pl.ds(r0, L), sl], kt_ref[0, pl.ds(r0, L), sl]],
                                 axis=0)
            upd = _dot(uv.T.astype(BF16), bk)
            z_s[gi] = zt * dl_ref[0, pl.ds(d0, 1), sl] + jnp.where(same_head, upd, 0.0)

            mu = _group_head_sum(y, masks) * (1.0 / R_HD)
            yc = y - mu
            var = _group_head_sum(yc * yc, masks) * (1.0 / R_HD)
            out_ref[0, pl.ds(r0, L), sl] = (yc * lax.rsqrt(var + RWKV_GN_EPS)
                                            * e1_ref[0, pl.ds(r0, L), sl]
                                            + e2_ref[0, pl.ds(r0, L), sl])
        return carry

    lax.fori_loop(0, n_chunks, chunk, 0)


def _rwkv(parts, rkp, gng, gnb, ts_intra=128, ts_seq=512):
    bsz, s, _ = parts[0].shape
    vec = pl.BlockSpec((1, R_WIDTH), lambda b, t: (0, 0))
    blk = pl.BlockSpec((1, ts_intra, R_WIDTH), lambda b, t: (b, t, 0))
    sds = lambda dt: jax.ShapeDtypeStruct((bsz, s, R_WIDTH), dt)
    ah, rq, be, kt, uh, y0, e1, e2, dl = pl.pallas_call(
        functools.partial(_rwkv_intra_kernel, n_chunks=ts_intra // CHUNK),
        grid=(bsz, s // ts_intra),
        in_specs=[blk] * 7 + [vec] * 3,
        out_specs=[blk] * 8 + [pl.BlockSpec((1, ts_intra // 8, R_WIDTH), lambda b, t: (b, t, 0))],
        out_shape=[sds(BF16)] * 4 + [sds(F32)] * 4
        + [jax.ShapeDtypeStruct((bsz, s // 8, R_WIDTH), F32)],
        compiler_params=_cparams(("parallel", "parallel")),
    )(*parts, rkp, gng, gnb)
    blk = pl.BlockSpec((1, ts_seq, R_WIDTH), lambda b, t: (b, t, 0))
    return pl.pallas_call(
        functools.partial(_rwkv_seq_kernel, n_chunks=ts_seq // CHUNK),
        grid=(bsz, s // ts_seq),
        in_specs=[blk] * 9 + [pl.BlockSpec((1, ts_seq // 8, R_WIDTH), lambda b, t: (b, t, 0))],
        out_specs=blk,
        out_shape=sds(F32),
        scratch_shapes=[pltpu.VMEM((R_WIDTH // R_GROUP, R_GROUP, R_GROUP), F32)],
        compiler_params=_cparams(("parallel", "arbitrary")),
    )(ah, rq, be, kt, parts[2], uh, y0, e1, e2, dl)


def _xattn_kernel(h_ref, kv_ref, wq_ref, wo_ref, g_ref, b_ref, o_ref):
    hq = h_ref[0]
    q = _dot(hq.astype(BF16), wq_ref[...])
    acc = jnp.zeros_like(hq)
    for hd in range(X_HEADS):
        sl = slice(hd * X_HD, (hd + 1) * X_HD)
        kh = kv_ref[0, :, sl]
        vh = kv_ref[0, :, D_MODEL + hd * X_HD:D_MODEL + (hd + 1) * X_HD]
        s = _dot_nt(q[:, sl].astype(BF16), kh) * (X_HD ** -0.5)
        e = jnp.exp(s - jnp.max(s, axis=1, keepdims=True))
        p = e / jnp.sum(e, axis=1, keepdims=True)
        oh = _dot(p.astype(BF16), vh)
        acc = acc + _dot(oh.astype(BF16), wo_ref[sl, :])
    o_ref[0] = _layer_norm(DN_ALPHA * hq + acc, g_ref[...], b_ref[...])


def _xattn(h3, kv3, wq, wo, g, b, tq=512):
    bsz, s, d = h3.shape
    m = kv3.shape[1]
    return pl.pallas_call(
        _xattn_kernel,
        grid=(bsz, s // tq),
        in_specs=[pl.BlockSpec((1, tq, d), lambda i, t: (i, t, 0)),
                  pl.BlockSpec((1, m, 2 * d), lambda i, t: (i, 0, 0)),
                  pl.BlockSpec(wq.shape, lambda i, t: (0, 0)),
                  pl.BlockSpec(wo.shape, lambda i, t: (0, 0)),
                  pl.BlockSpec((1, d), lambda i, t: (0, 0)),
                  pl.BlockSpec((1, d), lambda i, t: (0, 0))],
        out_specs=pl.BlockSpec((1, tq, d), lambda i, t: (i, t, 0)),
        out_shape=jax.ShapeDtypeStruct((bsz, s, d), F32),
        compiler_params=_cparams(("parallel", "parallel")),
    )(h3, kv3, wq, wo, g.reshape(1, d), b.reshape(1, d))


def _router_kernel(h_ref, wr_ref, br_ref, idx_o, gate_o, rank_o, meta_o, blk_o, cnt_s,
                   *, n_blocks):
    tm = h_ref.shape[0]
    step = pl.program_id(0)

    @pl.when(step == 0)
    def _():
        cnt_s[...] = jnp.zeros_like(cnt_s)

    lane = lax.broadcasted_iota(jnp.int32, (1, LANES), 1)
    logits = _dot_hi(h_ref[...], wr_ref[...]) + br_ref[...]
    logits = jnp.where(lane < N_EXPERTS, logits, -jnp.inf)
    vals, sels = [], []
    idx_out = jnp.zeros((tm, LANES), jnp.int32)
    for j in range(TOP_K):
        mx = jnp.max(logits, axis=1, keepdims=True)
        ij = jnp.min(jnp.where(logits == mx, lane, LANES), axis=1, keepdims=True)
        sel = lane == ij
        vals.append(mx)
        sels.append(sel)
        idx_out = jnp.where(lane == j, ij, idx_out)
        logits = jnp.where(sel, -jnp.inf, logits)
    exps = [jnp.exp(vj - vals[0]) for vj in vals]
    denom = exps[0] + exps[1] + exps[2] + exps[3]
    gate_out = jnp.zeros((tm, LANES), F32)
    for j in range(TOP_K):
        gate_out = jnp.where(lane == j, exps[j] / denom, gate_out)

    cnt = jnp.zeros((tm, LANES), F32)
    for sel in sels:
        cnt = cnt + sel.astype(F32)
    row = lax.broadcasted_iota(jnp.int32, (tm, tm), 0)
    col = lax.broadcasted_iota(jnp.int32, (tm, tm), 1)
    before = (row > col).astype(BF16)
    excl = _dot(before, cnt.astype(BF16)) + cnt_s[0:1, :]
    rank_out = jnp.zeros((tm, LANES), jnp.int32)
    for j in range(TOP_K):
        rj = jnp.sum(jnp.where(sels[j], excl, 0.0), axis=1, keepdims=True)
        rank_out = jnp.where(lane == j, rj.astype(jnp.int32), rank_out)
    total = cnt_s[0:1, :] + jnp.sum(cnt, axis=0, keepdims=True)
    cnt_s[...] = jnp.broadcast_to(total, cnt_s.shape)

    idx_o[...] = idx_out
    gate_o[...] = gate_out
    rank_o[...] = rank_out

    padded = jnp.floor((total + (MOE_ROWS - 1)) * (1.0 / MOE_ROWS)) * MOE_ROWS
    er = lax.broadcasted_iota(jnp.int32, (LANES, LANES), 0)
    ec = lax.broadcasted_iota(jnp.int32, (LANES, LANES), 1)
    pad_end = _dot_hi(padded, (er <= ec).astype(F32))
    pad_start = pad_end - padded
    meta = jnp.where(lax.broadcasted_iota(jnp.int32, (8, LANES), 0) == 0, pad_start, pad_end)
    meta_o[...] = meta.astype(jnp.int32)
    brow = lax.broadcasted_iota(jnp.int32, (n_blocks, LANES), 0).astype(F32) * MOE_ROWS
    done = jnp.where((lane < N_EXPERTS) & (pad_end <= brow), 1.0, 0.0)
    be = jnp.minimum(jnp.sum(done, axis=1, keepdims=True), N_EXPERTS - 1.0)
    blk_o[...] = jnp.broadcast_to(be, (n_blocks, LANES)).astype(jnp.int32)


def _router(h, wr, br, n_blocks, tm=512):
    n, d = h.shape
    kern = functools.partial(_router_kernel, n_blocks=n_blocks)
    tok = pl.BlockSpec((tm, LANES), lambda i: (i, 0))
    return pl.pallas_call(
        kern,
        grid=(n // tm,),
        in_specs=[pl.BlockSpec((tm, d), lambda i: (i, 0)),
                  pl.BlockSpec(wr.shape, lambda i: (0, 0)),
                  pl.BlockSpec((1, LANES), lambda i: (0, 0))],
        out_specs=[tok, tok, tok,
                   pl.BlockSpec((8, LANES), lambda i: (0, 0)),
                   pl.BlockSpec((n_blocks, LANES), lambda i: (0, 0))],
        out_shape=[jax.ShapeDtypeStruct((n, LANES), jnp.int32),
                   jax.ShapeDtypeStruct((n, LANES), F32),
                   jax.ShapeDtypeStruct((n, LANES), jnp.int32),
                   jax.ShapeDtypeStruct((8, LANES), jnp.int32),
                   jax.ShapeDtypeStruct((n_blocks, LANES), jnp.int32)],
        scratch_shapes=[pltpu.VMEM((8, LANES), F32)],
        compiler_params=_cparams(("arbitrary",)),
    )(h, wr, br)


def _dispatch_kernel(idx_ref, rank_ref, start_ref, x_ref, buf_in, buf_ref, sem):
    del buf_in
    tm = x_ref.shape[0]

    def copy(i):
        dest = start_ref[idx_ref[i]] + rank_ref[i]
        return pltpu.make_async_copy(x_ref.at[pl.ds(i // TOP_K, 1), :],
                                     buf_ref.at[pl.ds(dest, 1), :], sem)

    def issue(i, c):
        copy(i).start()
        return c

    def drain(i, c):
        copy(i).wait()
        return c

    lax.fori_loop(0, tm * TOP_K, issue, 0, unroll=DMA_UNROLL)
    lax.fori_loop(0, tm * TOP_K, drain, 0, unroll=DMA_UNROLL)


def _dispatch(h, idx_flat, rank_flat, pad_start, n_rows, tm=256):
    n, d = h.shape
    flat = pl.BlockSpec((tm * TOP_K,), lambda i: (i,), memory_space=pltpu.SMEM)
    return pl.pallas_call(
        _dispatch_kernel,
        grid=(n // tm,),
        in_specs=[flat, flat,
                  pl.BlockSpec(memory_space=pltpu.SMEM),
                  pl.BlockSpec((tm, d), lambda i: (i, 0)),
                  pl.BlockSpec(memory_space=pl.ANY)],
        out_specs=pl.BlockSpec(memory_space=pl.ANY),
        out_shape=jax.ShapeDtypeStruct((n_rows, d), F32),
        scratch_shapes=[pltpu.SemaphoreType.DMA(())],
        input_output_aliases={4: 0},
        compiler_params=_cparams(("arbitrary",)),
    )(idx_flat, rank_flat, pad_start, h, jnp.zeros((n_rows, d), F32))


def _expert_kernel(be_ref, nb_ref, x_ref, w1_ref, b1_ref, w2_ref, b2_ref, y_ref):
    del be_ref
    used = pl.program_id(0) < nb_ref[0]

    @pl.when(jnp.logical_not(used))
    def _():
        y_ref[...] = jnp.zeros_like(y_ref)

    @pl.when(used)
    def _():
        hu = _dot(x_ref[...].astype(BF16), w1_ref[0]) + b1_ref[0]
        gate = jnp.minimum(hu[:, :D_FF], SWIGLU_LIMIT)
        up = jnp.clip(hu[:, D_FF:], -SWIGLU_LIMIT, SWIGLU_LIMIT)
        glu = gate * _sigmoid(gate * SWIGLU_ALPHA)
        y_ref[...] = _dot(((up + 1.0) * glu).astype(BF16), w2_ref[0]) + b2_ref[0]


def _experts(buf, block_e, n_used, w1, b1, w2, b2):
    n_rows, d = buf.shape
    n_blocks = n_rows // MOE_ROWS
    rows = lambda i, be, nb: (jnp.minimum(i, nb[0] - 1), 0)
    wsel = lambda i, be, nb: (be[jnp.minimum(i, nb[0] - 1)], 0, 0)
    return pl.pallas_call(
        _expert_kernel,
        grid_spec=pltpu.PrefetchScalarGridSpec(
            num_scalar_prefetch=2,
            grid=(n_blocks,),
            in_specs=[pl.BlockSpec((MOE_ROWS, d), rows),
                      pl.BlockSpec((1, d, 2 * D_FF), wsel),
                      pl.BlockSpec((1, 1, 2 * D_FF), wsel),
                      pl.BlockSpec((1, D_FF, d), wsel),
                      pl.BlockSpec((1, 1, d), wsel)],
            out_specs=pl.BlockSpec((MOE_ROWS, d), lambda i, be, nb: (i, 0))),
        out_shape=jax.ShapeDtypeStruct((n_rows, d), F32),
        compiler_params=_cparams(("arbitrary",)),
    )(block_e, n_used, buf, w1, b1, w2, b2)


def _combine_kernel(idx_ref, rank_ref, start_ref, gate_ref, h_ref, g_ref, b_ref, y_hbm, o_ref,
                    gbuf, sem):
    tm = h_ref.shape[0]

    def copy(i):
        src = start_ref[idx_ref[i]] + rank_ref[i]
        return pltpu.make_async_copy(y_hbm.at[pl.ds(src, 1), :],
                                     gbuf.at[i % TOP_K, pl.ds(i // TOP_K, 1), :], sem)

    def issue(i, c):
        copy(i).start()
        return c

    def drain(i, c):
        copy(i).wait()
        return c

    lax.fori_loop(0, tm * TOP_K, issue, 0, unroll=DMA_UNROLL)
    lax.fori_loop(0, tm * TOP_K, drain, 0, unroll=DMA_UNROLL)
    moe = gate_ref[:, 0:1] * gbuf[0]
    for j in range(1, TOP_K):
        moe = moe + gate_ref[:, j:j + 1] * gbuf[j]
    o_ref[...] = _layer_norm(DN_ALPHA * h_ref[...] + moe, g_ref[...], b_ref[...])


def _combine(h, y_buf, idx_flat, rank_flat, pad_start, gates, g, b, tm=256):
    n, d = h.shape
    flat = pl.BlockSpec((tm * TOP_K,), lambda i: (i,), memory_space=pltpu.SMEM)
    return pl.pallas_call(
        _combine_kernel,
        grid=(n // tm,),
        in_specs=[flat, flat,
                  pl.BlockSpec(memory_space=pltpu.SMEM),
                  pl.BlockSpec((tm, LANES), lambda i: (i, 0)),
                  pl.BlockSpec((tm, d), lambda i: (i, 0)),
                  pl.BlockSpec((1, d), lambda i: (0, 0)),
                  pl.BlockSpec((1, d), lambda i: (0, 0)),
                  pl.BlockSpec(memory_space=pl.ANY)],
        out_specs=pl.BlockSpec((tm, d), lambda i: (i, 0)),
        out_shape=jax.ShapeDtypeStruct((n, d), F32),
        scratch_shapes=[pltpu.VMEM((TOP_K, tm, d), F32), pltpu.SemaphoreType.DMA(())],
        compiler_params=_cparams(("arbitrary",)),
    )(idx_flat, rank_flat, pad_start, gates, h, g.reshape(1, d), b.reshape(1, d), y_buf)


def _moe(h, wr, br, w1, b1, w2, b2, g, b):
    n, d = h.shape
    n_blocks = n * TOP_K // MOE_ROWS + N_EXPERTS
    wr_pad = jnp.pad(wr, ((0, 0), (0, LANES - N_EXPERTS)))
    br_pad = jnp.pad(br, (0, LANES - N_EXPERTS)).reshape(1, LANES)
    idx, gates, rank, meta, blk = _router(h, wr_pad, br_pad, n_blocks)
    idx_flat = idx[:, :TOP_K].reshape(-1)
    rank_flat = rank[:, :TOP_K].reshape(-1)
    pad_start = meta[0, :N_EXPERTS]
    n_used = meta[1, N_EXPERTS - 1:N_EXPERTS] // MOE_ROWS
    buf = _dispatch(h, idx_flat, rank_flat, pad_start, n_blocks * MOE_ROWS)
    y_buf = _experts(buf, blk[:, 0], n_used, w1.astype(BF16), b1[:, None, :],
                     w2.astype(BF16), b2[:, None, :])
    return _combine(h, y_buf, idx_flat, rank_flat, pad_start, gates, g, b)


def _block_diag_ones(width, head):
    r = jnp.arange(width) // head
    return (r[:, None] == r[None, :]).astype(BF16)


def _mixer_layer(h, bsz, s, p, l, debug=False):
    m_cols = 4 * M_WIDTH + 2 * M_HEADS
    rw = 3 * R_WIDTH
    row = lambda a: a.reshape(1, -1)
    n = bsz * s
    wl = p['w_in'][l]
    w_m = jnp.pad(wl[:, :m_cols], ((0, 0), (0, M_COLS_PAD - m_cols))).astype(BF16)
    perm = lambda a: jnp.concatenate([a[..., :rw], a[..., rw + 128:], a[..., rw:rw + 128]], axis=-1)
    w_r = perm(wl[:, m_cols:]).astype(BF16)
    proj_m, proj_r = _proj(h, w_m, w_r)

    gate_bias = jnp.pad(jnp.concatenate([p['m_ig_b'][l], p['m_fg_b'][l]]),
                        (0, LANES - 2 * M_HEADS))
    hm = _mlstm(proj_m.reshape(bsz, s, M_COLS_PAD), p['m_conv_w'][l], row(p['m_conv_b'][l]),
                row(gate_bias), row(p['m_norm_g'][l]))

    zeros = jnp.zeros((R_WIDTH // 8, R_WIDTH), F32)
    w_lr = jnp.concatenate([jnp.concatenate([p['r_w2'][l], zeros], axis=1),
                            jnp.concatenate([zeros, p['r_a2'][l]], axis=1)], axis=0).astype(BF16)
    parts = _rwkv_prep(proj_r.reshape(bsz, s, R_COLS), row(perm(p['r_mu'][l])), row(p['r_w0'][l]),
                       row(p['r_a0'][l]), w_lr, p['r_g2'][l].astype(BF16), row(p['r_kk'][l]),
                       row(p['r_ka'][l]), _block_diag_ones(R_WIDTH, R_HD))
    hr = _rwkv(parts, row(p['r_rk'][l]), row(p['r_gn_g'][l]), row(p['r_gn_b'][l]))

    h1 = _mix_out(hm.reshape(n, M_WIDTH), hr.reshape(n, R_WIDTH), h, p['w_out'][l].astype(BF16),
                  p['ln1_g'][l], p['ln1_b'][l])
    if debug:
        return h1, hm, hr
    return h1


def _xattn_layer(h, mem2, bsz, s, p, l):
    n, d = h.shape
    n_mem = mem2.shape[0] // bsz
    kv = _mm(mem2, p['x_wkv'][l].astype(BF16), BF16)
    return _xattn(h.reshape(bsz, s, d), kv.reshape(bsz, n_mem, 2 * d), p['x_wq'][l].astype(BF16),
                  p['x_wo'][l].astype(BF16), p['ln2_g'][l], p['ln2_b'][l]).reshape(n, d)


def kernel(x, mem, ln0_g, ln0_b, w_in, m_conv_w, m_conv_b, m_ig_b, m_fg_b, m_norm_g, r_mu, r_w0,
           r_w2, r_a0, r_a2, r_g2, r_kk, r_ka, r_rk, r_gn_g, r_gn_b, w_out, ln1_g, ln1_b, x_wq,
           x_wkv, x_wo, ln2_g, ln2_b, moe_wr, moe_br, moe_w1, moe_b1, moe_w2, moe_b2, ln3_g,
           ln3_b):
    p = dict(w_in=w_in, m_conv_w=m_conv_w, m_conv_b=m_conv_b, m_ig_b=m_ig_b, m_fg_b=m_fg_b,
             m_norm_g=m_norm_g, r_mu=r_mu, r_w0=r_w0, r_w2=r_w2, r_a0=r_a0, r_a2=r_a2, r_g2=r_g2,
             r_kk=r_kk, r_ka=r_ka, r_rk=r_rk, r_gn_g=r_gn_g, r_gn_b=r_gn_b, w_out=w_out,
             ln1_g=ln1_g, ln1_b=ln1_b, x_wq=x_wq, x_wkv=x_wkv, x_wo=x_wo, ln2_g=ln2_g,
             ln2_b=ln2_b)
    bsz, s, d = x.shape
    n = bsz * s
    h = _ln(x.reshape(n, d), ln0_g, ln0_b)
    mem2 = mem.reshape(bsz * mem.shape[1], d)
    for l in range(DEPTH):
        h = _mixer_layer(h, bsz, s, p, l)
        h = _xattn_layer(h, mem2, bsz, s, p, l)
        h = _moe(h, moe_wr[l], moe_br[l], moe_w1[l], moe_b1[l], moe_w2[l], moe_b2[l],
                 ln3_g[l], ln3_b[l])
    return h.reshape(bsz, s, d)
```

```python
import functools

import jax
import jax.numpy as jnp
from jax import lax
from jax.experimental import pallas as pl
from jax.experimental.pallas import tpu as pltpu

F32 = jnp.float32
BF16 = jnp.bfloat16
HIGHEST = lax.Precision.HIGHEST

D_MODEL = 1024
DEPTH = 2
CHUNK = 64
M_WIDTH = 512
M_HEADS = 4
M_HD = 128
M_CONV = 4
R_WIDTH = 512
R_HD = 64
R_HEADS = 8
R_LR = 256
X_HEADS = 4
X_HD = 256
N_EXPERTS = 32
TOP_K = 4
D_FF = 1024
SWIGLU_LIMIT = 7.0
SWIGLU_ALPHA = 1.702
LN_EPS = 1e-5
HEAD_NORM_EPS = 1e-5
RWKV_GN_EPS = 64e-5
DN_ALPHA = (2 * DEPTH) ** 0.25

LANES = 128
M_COLS_PAD = 4 * M_WIDTH + LANES
R_COLS = 3 * R_WIDTH + R_LR
R_GROUP = 256
DMA_UNROLL = 8
MOE_ROWS = 256
VMEM_LIMIT = 56 * 1024 * 1024


def _cparams(sem, vmem=VMEM_LIMIT):
    return pltpu.CompilerParams(dimension_semantics=sem, vmem_limit_bytes=vmem)


def _dot(a, b):
    return jnp.dot(a, b, preferred_element_type=F32)


def _dot_nt(a, b):
    return lax.dot_general(a, b, (((1,), (1,)), ((), ())), preferred_element_type=F32)


def _dot_hi(a, b):
    return jnp.dot(a, b, preferred_element_type=F32, precision=HIGHEST)


def _dot_nt_hi(a, b):
    return lax.dot_general(a, b, (((1,), (1,)), ((), ())), preferred_element_type=F32,
                           precision=HIGHEST)


def _layer_norm(x, g, b):
    mu = jnp.mean(x, axis=-1, keepdims=True)
    xc = x - mu
    var = jnp.mean(xc * xc, axis=-1, keepdims=True)
    return xc * lax.rsqrt(var + LN_EPS) * g + b


def _sigmoid(x):
    return 1.0 / (1.0 + jnp.exp(-x))


def _log_sigmoid(x):
    return jnp.minimum(x, 0.0) - jnp.log1p(jnp.exp(-jnp.abs(x)))


def _ln_kernel(x_ref, g_ref, b_ref, o_ref):
    o_ref[...] = _layer_norm(x_ref[...], g_ref[...], b_ref[...])


def _ln(x, g, b, tm=512):
    n, d = x.shape
    return pl.pallas_call(
        _ln_kernel,
        grid=(n // tm,),
        in_specs=[pl.BlockSpec((tm, d), lambda i: (i, 0)),
                  pl.BlockSpec((1, d), lambda i: (0, 0)),
                  pl.BlockSpec((1, d), lambda i: (0, 0))],
        out_specs=pl.BlockSpec((tm, d), lambda i: (i, 0)),
        out_shape=jax.ShapeDtypeStruct((n, d), F32),
        compiler_params=_cparams(("parallel",)),
    )(x, g.reshape(1, d), b.reshape(1, d))


def _proj_kernel(x_ref, wm_ref, wr_ref, om_ref, or_ref):
    x = x_ref[...].astype(BF16)
    om_ref[...] = _dot(x, wm_ref[...])
    or_ref[...] = _dot(x, wr_ref[...])


def _proj(h, wm, wr, tm=512):
    n, d = h.shape
    return pl.pallas_call(
        _proj_kernel,
        grid=(n // tm,),
        in_specs=[pl.BlockSpec((tm, d), lambda i: (i, 0)),
                  pl.BlockSpec(wm.shape, lambda i: (0, 0)),
                  pl.BlockSpec(wr.shape, lambda i: (0, 0))],
        out_specs=[pl.BlockSpec((tm, wm.shape[1]), lambda i: (i, 0)),
                   pl.BlockSpec((tm, wr.shape[1]), lambda i: (i, 0))],
        out_shape=[jax.ShapeDtypeStruct((n, wm.shape[1]), F32),
                   jax.ShapeDtypeStruct((n, wr.shape[1]), F32)],
        compiler_params=_cparams(("parallel",)),
    )(h, wm, wr)


def _mm_kernel(x_ref, w_ref, o_ref):
    o_ref[...] = _dot(x_ref[...].astype(BF16), w_ref[...]).astype(o_ref.dtype)


def _mm(x, w, out_dtype, tm=512):
    n, d = x.shape
    return pl.pallas_call(
        _mm_kernel,
        grid=(n // tm,),
        in_specs=[pl.BlockSpec((tm, d), lambda i: (i, 0)),
                  pl.BlockSpec(w.shape, lambda i: (0, 0))],
        out_specs=pl.BlockSpec((tm, w.shape[1]), lambda i: (i, 0)),
        out_shape=jax.ShapeDtypeStruct((n, w.shape[1]), out_dtype),
        compiler_params=_cparams(("parallel",)),
    )(x, w)


def _mix_out_kernel(hm_ref, hr_ref, h_ref, w_ref, g_ref, b_ref, o_ref):
    mix = _dot(hm_ref[...].astype(BF16), w_ref[:M_WIDTH, :])
    mix = mix + _dot(hr_ref[...].astype(BF16), w_ref[M_WIDTH:, :])
    o_ref[...] = _layer_norm(DN_ALPHA * h_ref[...] + mix, g_ref[...], b_ref[...])


def _mix_out(hm, hr, h, w, g, b, tm=512):
    n, d = h.shape
    return pl.pallas_call(
        _mix_out_kernel,
        grid=(n // tm,),
        in_specs=[pl.BlockSpec((tm, M_WIDTH), lambda i: (i, 0)),
                  pl.BlockSpec((tm, R_WIDTH), lambda i: (i, 0)),
                  pl.BlockSpec((tm, d), lambda i: (i, 0)),
                  pl.BlockSpec(w.shape, lambda i: (0, 0)),
                  pl.BlockSpec((1, d), lambda i: (0, 0)),
                  pl.BlockSpec((1, d), lambda i: (0, 0))],
        out_specs=pl.BlockSpec((tm, d), lambda i: (i, 0)),
        out_shape=jax.ShapeDtypeStruct((n, d), F32),
        compiler_params=_cparams(("parallel",)),
    )(hm, hr, h, w, g.reshape(1, d), b.reshape(1, d))


def _mlstm_kernel(qk_ref, v_ref, o_ref, gt_ref, cw_ref, cb_ref, gb_ref, ng_ref, out_ref,
                  ct_s, n_s, m_s, tail_s, *, n_chunks):
    L = CHUNK

    @pl.when(pl.program_id(1) == 0)
    def _():
        ct_s[...] = jnp.zeros_like(ct_s)
        n_s[...] = jnp.zeros_like(n_s)
        m_s[...] = jnp.zeros_like(m_s)
        tail_s[...] = jnp.zeros_like(tail_s)

    row = lax.broadcasted_iota(jnp.int32, (L, L), 0)
    col = lax.broadcasted_iota(jnp.int32, (L, L), 1)
    causal = row >= col
    tril_f = causal.astype(F32)
    lane = lax.broadcasted_iota(jnp.int32, (1, LANES), 1)

    def chunk(c, carry):
        r0 = pl.multiple_of(c * L, L)
        x = qk_ref[0, pl.ds(r0, L), :]
        cat = jnp.concatenate([tail_s[...], x], axis=0)
        acc = cb_ref[...] + cw_ref[M_CONV - 1:M_CONV, :] * x
        for j in range(M_CONV - 1):
            acc = acc + cw_ref[j:j + 1, :] * pltpu.roll(cat, M_CONV - 1 - j, axis=0)[8:, :]
        tail_s[...] = x[L - 8:, :]
        qk = acc * _sigmoid(acc)

        gx = gt_ref[0, pl.ds(r0, L), :] + gb_ref[...]
        bc = _dot_hi(tril_f, _log_sigmoid(gx))
        rows = jnp.where(lane < M_HEADS, gx, bc).T

        for h in range(M_HEADS):
            q = qk[:, h * M_HD:(h + 1) * M_HD]
            k = qk[:, M_WIDTH + h * M_HD:M_WIDTH + (h + 1) * M_HD] * (M_HD ** -0.5)
            v = v_ref[0, pl.ds(r0, L), h * M_HD:(h + 1) * M_HD]
            ig_c = gx[:, h:h + 1]
            b_c = bc[:, M_HEADS + h:M_HEADS + h + 1]
            ig_r = rows[h:h + 1, :]
            b_r = rows[M_HEADS + h:M_HEADS + h + 1, :]
            b_last = bc[L - 1:L, M_HEADS + h:M_HEADS + h + 1]
            m_prev = m_s[h, 0:1, 0:1]
            n_prev = n_s[h, 0:1, :]
            ct_prev = ct_s[h]

            qb = q.astype(BF16)
            kb = k.astype(BF16)
            vb = v.astype(BF16)

            g_inter = b_c + m_prev
            d_mat = jnp.where(causal, b_c - b_r + ig_r, -jnp.inf)
            m_t = jnp.maximum(g_inter, jnp.max(d_mat, axis=1, keepdims=True))
            s = _dot_nt(qb, kb) * jnp.exp(d_mat - m_t)
            w_inter = jnp.exp(g_inter - m_t)
            num = _dot(s.astype(BF16), vb) + w_inter * _dot(qb, ct_prev.astype(BF16))
            den = (jnp.sum(s, axis=1, keepdims=True)
                   + w_inter * jnp.sum(q * n_prev, axis=1, keepdims=True))
            hc = num / jnp.maximum(jnp.abs(den), jnp.exp(-m_t))
            mu = jnp.mean(hc, axis=1, keepdims=True)
            hcc = hc - mu
            var = jnp.mean(hcc * hcc, axis=1, keepdims=True)
            og = o_ref[0, pl.ds(r0, L), h * M_HD:(h + 1) * M_HD]
            out = hcc * lax.rsqrt(var + HEAD_NORM_EPS) * ng_ref[:, h * M_HD:(h + 1) * M_HD]
            out_ref[0, pl.ds(r0, L), h * M_HD:(h + 1) * M_HD] = out * _sigmoid(og)

            src = b_last - b_c + ig_c
            m_loc = jnp.max(src, axis=0, keepdims=True)
            kw = k * jnp.exp(src - m_loc)
            ct_loc = _dot(kw.T.astype(BF16), vb)
            n_loc = jnp.sum(kw, axis=0, keepdims=True)
            m_new = jnp.maximum(b_last + m_prev, m_loc)
            a = jnp.exp(b_last + m_prev - m_new)
            cc = jnp.exp(m_loc - m_new)
            ct_s[h] = a * ct_prev + cc * ct_loc
            n_s[h] = jnp.broadcast_to(a * n_prev + cc * n_loc, (8, M_HD))
            m_s[h] = jnp.broadcast_to(m_new, (8, LANES))
        return carry

    lax.fori_loop(0, n_chunks, chunk, 0)


def _mlstm(proj_m, cw, cb, gb, ng, ts=512):
    bsz, s, _ = proj_m.shape
    kern = functools.partial(_mlstm_kernel, n_chunks=ts // CHUNK)
    return pl.pallas_call(
        kern,
        grid=(bsz, s // ts),
        in_specs=[pl.BlockSpec((1, ts, 2 * M_WIDTH), lambda b, t: (b, t, 0)),
                  pl.BlockSpec((1, ts, M_WIDTH), lambda b, t: (b, t, 2)),
                  pl.BlockSpec((1, ts, M_WIDTH), lambda b, t: (b, t, 3)),
                  pl.BlockSpec((1, ts, LANES), lambda b, t: (b, t, 4 * M_WIDTH // LANES)),
                  pl.BlockSpec((M_CONV, 2 * M_WIDTH), lambda b, t: (0, 0)),
                  pl.BlockSpec((1, 2 * M_WIDTH), lambda b, t: (0, 0)),
                  pl.BlockSpec((1, LANES), lambda b, t: (0, 0)),
                  pl.BlockSpec((1, M_WIDTH), lambda b, t: (0, 0))],
        out_specs=pl.BlockSpec((1, ts, M_WIDTH), lambda b, t: (b, t, 0)),
        out_shape=jax.ShapeDtypeStruct((bsz, s, M_WIDTH), F32),
        scratch_shapes=[pltpu.VMEM((M_HEADS, M_HD, M_HD), F32),
                        pltpu.VMEM((M_HEADS, 8, M_HD), F32),
                        pltpu.VMEM((M_HEADS, 8, LANES), F32),
                        pltpu.VMEM((8, 2 * M_WIDTH), F32)],
        compiler_params=_cparams(("parallel", "arbitrary")),
    )(proj_m, proj_m, proj_m, proj_m, cw, cb, gb, ng)


def _rwkv_prep_kernel(p_ref, pp_ref, mu_ref, w0_ref, a0_ref, wlr_ref, g2_ref, kkp_ref, ka_ref,
                      bd_ref, r_o, k_o, v_o, kk_o, a_o, ld_o, g_o):
    ts = p_ref.shape[1]
    p = p_ref[0]
    first = pl.program_id(1) == 0
    prev_last = jnp.where(first, 0.0, pp_ref[0, 7:8, :])
    row = lax.broadcasted_iota(jnp.int32, (ts, 1), 0)
    prev = jnp.where(row == 0, prev_last, pltpu.roll(p, 1, axis=0))
    xs = p + (prev - p) * mu_ref[...]
    rr = xs[:, :R_WIDTH]
    rk = xs[:, R_WIDTH:2 * R_WIDTH]
    rv = xs[:, 2 * R_WIDTH:3 * R_WIDTH]
    gl = xs[:, 3 * R_WIDTH:3 * R_WIDTH + LANES]
    wa = xs[:, 3 * R_WIDTH + LANES:]
    lane = lax.broadcasted_iota(jnp.int32, (1, LANES), 1)
    wa = jnp.where(lane < LANES // 2, jnp.tanh(wa), wa)
    wa_out = _dot(wa.astype(BF16), wlr_ref[...])
    w_log = _log_sigmoid(w0_ref[...] + wa_out[:, :R_WIDTH]) - 0.5
    a = _sigmoid(a0_ref[...] + wa_out[:, R_WIDTH:])
    kk = rk * kkp_ref[...]
    sq = kk * kk
    sq_hi = sq.astype(BF16)
    sq_lo = (sq - sq_hi.astype(F32)).astype(BF16)
    ss = _dot(sq_hi, bd_ref[...]) + _dot(sq_lo, bd_ref[...])
    r_o[0] = rr
    k_o[0] = rk * (1.0 + (a - 1.0) * ka_ref[...])
    v_o[0] = rv
    kk_o[0] = kk / jnp.maximum(jnp.sqrt(ss), 1e-12)
    a_o[0] = a
    ld_o[0] = -jnp.exp(w_log)
    g_o[0] = _dot(_sigmoid(gl).astype(BF16), g2_ref[...])


def _rwkv_prep(proj_r, mu, w0, a0, wlr, g2, kkp, ka, bd, ts=256):
    bsz, s, _ = proj_r.shape
    vec = lambda w: pl.BlockSpec((1, w), lambda b, t: (0, 0))
    full = lambda a: pl.BlockSpec(a.shape, lambda b, t: (0, 0))
    out_spec = pl.BlockSpec((1, ts, R_WIDTH), lambda b, t: (b, t, 0))
    out_sds = jax.ShapeDtypeStruct((bsz, s, R_WIDTH), F32)
    return pl.pallas_call(
        _rwkv_prep_kernel,
        grid=(bsz, s // ts),
        in_specs=[pl.BlockSpec((1, ts, R_COLS), lambda b, t: (b, t, 0)),
                  pl.BlockSpec((1, 8, R_COLS),
                               lambda b, t: (b, jnp.maximum(t * (ts // 8) - 1, 0), 0)),
                  vec(R_COLS), vec(R_WIDTH), vec(R_WIDTH), full(wlr), full(g2),
                  vec(R_WIDTH), vec(R_WIDTH), full(bd)],
        out_specs=[out_spec] * 7,
        out_shape=[out_sds] * 7,
        compiler_params=_cparams(("parallel", "arbitrary")),
    )(proj_r, proj_r, mu, w0, a0, wlr, g2, kkp, ka, bd)


def _group_head_sum(x, masks):
    out = jnp.sum(jnp.where(masks[-1], x, 0.0), axis=1, keepdims=True)
    for m in masks[-2::-1]:
        out = jnp.where(m, jnp.sum(jnp.where(m, x, 0.0), axis=1, keepdims=True), out)
    return out


def _rwkv_intra_kernel(r_ref, k_ref, v_ref, kk_ref, a_ref, ld_ref, g_ref, rkp_ref, gng_ref,
                       gnb_ref, ah_o, rq_o, be_o, kt_o, uh_o, y0_o, e1_o, e2_o, dl_o, *, n_chunks):
    L = CHUNK
    G = R_GROUP
    row = lax.broadcasted_iota(jnp.int32, (L, L), 0)
    col = lax.broadcasted_iota(jnp.int32, (L, L), 1)
    tril_b = (row >= col).astype(BF16)
    br = lax.broadcasted_iota(jnp.int32, (G, G), 0)
    bc = lax.broadcasted_iota(jnp.int32, (G, G), 1)
    same = (br // L) == (bc // L)
    lower_s = same & ((br % L) > (bc % L))
    lower_i = same & ((br % L) >= (bc % L))
    eye = (br == bc).astype(F32)
    lane = lax.broadcasted_iota(jnp.int32, (1, G), 1)
    masks = [(lane // R_HD) == h for h in range(G // R_HD)]

    def stack(x):
        return jnp.concatenate([jnp.where(m, x, 0.0) for m in masks], axis=0).astype(BF16)

    def collapse(x):
        out = x[0:L]
        for h in range(1, G // R_HD):
            out = out + x[h * L:(h + 1) * L]
        return out

    for ci in range(n_chunks):
        rows = slice(ci * L, (ci + 1) * L)
        for gi in range(R_WIDTH // G):
            sl = slice(gi * G, (gi + 1) * G)
            r = r_ref[0, rows, sl]
            k = k_ref[0, rows, sl]
            v = v_ref[0, rows, sl]
            kk = kk_ref[0, rows, sl]
            a = a_ref[0, rows, sl]
            ld = ld_ref[0, rows, sl]

            ld_hi = ld.astype(BF16)
            ld_lo = (ld - ld_hi.astype(F32)).astype(BF16)
            cum = _dot(tril_b, ld_hi) + _dot(tril_b, ld_lo)
            dec = jnp.exp(cum)
            inv = jnp.exp(-cum)
            al = -kk * jnp.exp(cum - ld)
            be = kk * a * inv
            kt = k * inv
            rt = r * dec
            dl = dec[L - 1:L, :]

            al4 = stack(al)
            v4 = stack(v)
            big = _dot_nt(jnp.concatenate([al4, stack(rt)], axis=0),
                          jnp.concatenate([stack(be), stack(kt)], axis=0))
            ab = jnp.where(lower_s, big[:G, :G], 0.0)
            ak = jnp.where(lower_s, big[:G, G:], 0.0)
            rb = jnp.where(lower_i, big[G:, :G], 0.0).astype(BF16)
            rk = jnp.where(lower_i, big[G:, G:], 0.0).astype(BF16)
            w = eye + ab
            xp = ab
            for _ in range(5):
                xpb = xp.astype(BF16)
                xp = _dot(xpb, xpb)
                w = w + _dot(w.astype(BF16), xp.astype(BF16))
            wb = w.astype(BF16)
            ahs = _dot(wb, al4)
            uhs = _dot(wb, _dot(ak.astype(BF16), v4).astype(BF16))
            rqs = _dot(rb, ahs.astype(BF16))
            y0s = _dot(rb, uhs.astype(BF16)) + _dot(rk, v4)

            ah_o[0, rows, sl] = collapse(ahs).astype(BF16)
            rq_o[0, rows, sl] = (rt + collapse(rqs)).astype(BF16)
            be_o[0, rows, sl] = (be * dl).astype(BF16)
            kt_o[0, rows, sl] = (kt * dl).astype(BF16)
            uh_o[0, rows, sl] = collapse(uhs)
            y0_o[0, rows, sl] = collapse(y0s)
            g = g_ref[0, rows, sl]
            bonus = _group_head_sum(r * k * rkp_ref[:, sl], masks) * v
            e1_o[0, rows, sl] = g * gng_ref[:, sl]
            e2_o[0, rows, sl] = (gnb_ref[:, sl] + bonus) * g
            dl_o[0, ci * 8:(ci + 1) * 8, sl] = jnp.broadcast_to(dl, (8, G))


def _rwkv_seq_kernel(ah_ref, rq_ref, be_ref, kt_ref, v_ref, uh_ref, y0_ref, e1_ref, e2_ref, dl_ref,
                     out_ref, z_s, *, n_chunks):
    L = CHUNK
    G = R_GROUP

    @pl.when(pl.program_id(1) == 0)
    def _():
        z_s[...] = jnp.zeros_like(z_s)

    zr = lax.broadcasted_iota(jnp.int32, (G, G), 0) // R_HD
    zc = lax.broadcasted_iota(jnp.int32, (G, G), 1) // R_HD
    same_head = zr == zc
    lane = lax.broadcasted_iota(jnp.int32, (1, G), 1)
    masks = [(lane // R_HD) == h for h in range(G // R_HD)]

    def chunk(c, carry):
        r0 = pl.multiple_of(c * L, L)
        d0 = pl.multiple_of(c * 8, 8)
        for gi in range(R_WIDTH // G):
            sl = slice(gi * G, (gi + 1) * G)
            zt = z_s[gi]
            lhs = jnp.concatenate([ah_ref[0, pl.ds(r0, L), sl], rq_ref[0, pl.ds(r0, L), sl]],
                                  axis=0)
            pr = _dot_nt(lhs, zt.astype(BF16))
            u = pr[:L] + uh_ref[0, pl.ds(r0, L), sl]
            y = pr[L:] + y0_ref[0, pl.ds(r0, L), sl]
            uv = jnp.concatenate([u, v_ref[0, pl.ds(r0, L), sl]], axis=0)
            bk = jnp.concatenate([be_ref[0, pl.ds(r0, L), sl], kt_ref[0, pl.ds(r0, L), sl]],
                                 axis=0)
            upd = _dot(uv.T.astype(BF16), bk)
            z_s[gi] = zt * dl_ref[0, pl.ds(d0, 1), sl] + jnp.where(same_head, upd, 0.0)

            mu = _group_head_sum(y, masks) * (1.0 / R_HD)
            yc = y - mu
            var = _group_head_sum(yc * yc, masks) * (1.0 / R_HD)
            out_ref[0, pl.ds(r0, L), sl] = (yc * lax.rsqrt(var + RWKV_GN_EPS)
                                            * e1_ref[0, pl.ds(r0, L), sl]
                                            + e2_ref[0, pl.ds(r0, L), sl])
        return carry

    lax.fori_loop(0, n_chunks, chunk, 0)


def _rwkv(parts, rkp, gng, gnb, ts_intra=128, ts_seq=512):
    bsz, s, _ = parts[0].shape
    vec = pl.BlockSpec((1, R_WIDTH), lambda b, t: (0, 0))
    blk = pl.BlockSpec((1, ts_intra, R_WIDTH), lambda b, t: (b, t, 0))
    sds = lambda dt: jax.ShapeDtypeStruct((bsz, s, R_WIDTH), dt)
    ah, rq, be, kt, uh, y0, e1, e2, dl = pl.pallas_call(
        functools.partial(_rwkv_intra_kernel, n_chunks=ts_intra // CHUNK),
        grid=(bsz, s // ts_intra),
        in_specs=[blk] * 7 + [vec] * 3,
        out_specs=[blk] * 8 + [pl.BlockSpec((1, ts_intra // 8, R_WIDTH), lambda b, t: (b, t, 0))],
        out_shape=[sds(BF16)] * 4 + [sds(F32)] * 4
        + [jax.ShapeDtypeStruct((bsz, s // 8, R_WIDTH), F32)],
        compiler_params=_cparams(("parallel", "parallel")),
    )(*parts, rkp, gng, gnb)
    blk = pl.BlockSpec((1, ts_seq, R_WIDTH), lambda b, t: (b, t, 0))
    return pl.pallas_call(
        functools.partial(_rwkv_seq_kernel, n_chunks=ts_seq // CHUNK),
        grid=(bsz, s // ts_seq),
        in_specs=[blk] * 9 + [pl.BlockSpec((1, ts_seq // 8, R_WIDTH), lambda b, t: (b, t, 0))],
        out_specs=blk,
        out_shape=sds(F32),
        scratch_shapes=[pltpu.VMEM((R_WIDTH // R_GROUP, R_GROUP, R_GROUP), F32)],
        compiler_params=_cparams(("parallel", "arbitrary")),
    )(ah, rq, be, kt, parts[2], uh, y0, e1, e2, dl)


def _xattn_kernel(h_ref, kv_ref, wq_ref, wo_ref, g_ref, b_ref, o_ref):
    hq = h_ref[0]
    q = _dot(hq.astype(BF16), wq_ref[...])
    acc = jnp.zeros_like(hq)
    for hd in range(X_HEADS):
        sl = slice(hd * X_HD, (hd + 1) * X_HD)
        kh = kv_ref[0, :, sl]
        vh = kv_ref[0, :, D_MODEL + hd * X_HD:D_MODEL + (hd + 1) * X_HD]
        s = _dot_nt(q[:, sl].astype(BF16), kh) * (X_HD ** -0.5)
        e = jnp.exp(s - jnp.max(s, axis=1, keepdims=True))
        p = e / jnp.sum(e, axis=1, keepdims=True)
        oh = _dot(p.astype(BF16), vh)
        acc = acc + _dot(oh.astype(BF16), wo_ref[sl, :])
    o_ref[0] = _layer_norm(DN_ALPHA * hq + acc, g_ref[...], b_ref[...])


def _xattn(h3, kv3, wq, wo, g, b, tq=512):
    bsz, s, d = h3.shape
    m = kv3.shape[1]
    return pl.pallas_call(
        _xattn_kernel,
        grid=(bsz, s // tq),
        in_specs=[pl.BlockSpec((1, tq, d), lambda i, t: (i, t, 0)),
                  pl.BlockSpec((1, m, 2 * d), lambda i, t: (i, 0, 0)),
                  pl.BlockSpec(wq.shape, lambda i, t: (0, 0)),
                  pl.BlockSpec(wo.shape, lambda i, t: (0, 0)),
                  pl.BlockSpec((1, d), lambda i, t: (0, 0)),
                  pl.BlockSpec((1, d), lambda i, t: (0, 0))],
        out_specs=pl.BlockSpec((1, tq, d), lambda i, t: (i, t, 0)),
        out_shape=jax.ShapeDtypeStruct((bsz, s, d), F32),
        compiler_params=_cparams(("parallel", "parallel")),
    )(h3, kv3, wq, wo, g.reshape(1, d), b.reshape(1, d))


def _router_kernel(h_ref, wr_ref, br_ref, idx_o, gate_o, rank_o, meta_o, blk_o, cnt_s,
                   *, n_blocks):
    tm = h_ref.shape[0]
    step = pl.program_id(0)

    @pl.when(step == 0)
    def _():
        cnt_s[...] = jnp.zeros_like(cnt_s)

    lane = lax.broadcasted_iota(jnp.int32, (1, LANES), 1)
    logits = _dot_hi(h_ref[...], wr_ref[...]) + br_ref[...]
    logits = jnp.where(lane < N_EXPERTS, logits, -jnp.inf)
    vals, sels = [], []
    idx_out = jnp.zeros((tm, LANES), jnp.int32)
    for j in range(TOP_K):
        mx = jnp.max(logits, axis=1, keepdims=True)
        ij = jnp.min(jnp.where(logits == mx, lane, LANES), axis=1, keepdims=True)
        sel = lane == ij
        vals.append(mx)
        sels.append(sel)
        idx_out = jnp.where(lane == j, ij, idx_out)
        logits = jnp.where(sel, -jnp.inf, logits)
    exps = [jnp.exp(vj - vals[0]) for vj in vals]
    denom = exps[0] + exps[1] + exps[2] + exps[3]
    gate_out = jnp.zeros((tm, LANES), F32)
    for j in range(TOP_K):
        gate_out = jnp.where(lane == j, exps[j] / denom, gate_out)

    cnt = jnp.zeros((tm, LANES), F32)
    for sel in sels:
        cnt = cnt + sel.astype(F32)
    row = lax.broadcasted_iota(jnp.int32, (tm, tm), 0)
    col = lax.broadcasted_iota(jnp.int32, (tm, tm), 1)
    before = (row > col).astype(BF16)
    excl = _dot(before, cnt.astype(BF16)) + cnt_s[0:1, :]
    rank_out = jnp.zeros((tm, LANES), jnp.int32)
    for j in range(TOP_K):
        rj = jnp.sum(jnp.where(sels[j], excl, 0.0), axis=1, keepdims=True)
        rank_out = jnp.where(lane == j, rj.astype(jnp.int32), rank_out)
    total = cnt_s[0:1, :] + jnp.sum(cnt, axis=0, keepdims=True)
    cnt_s[...] = jnp.broadcast_to(total, cnt_s.shape)

    idx_o[...] = idx_out
    gate_o[...] = gate_out
    rank_o[...] = rank_out

    padded = jnp.floor((total + (MOE_ROWS - 1)) * (1.0 / MOE_ROWS)) * MOE_ROWS
    er = lax.broadcasted_iota(jnp.int32, (LANES, LANES), 0)
    ec = lax.broadcasted_iota(jnp.int32, (LANES, LANES), 1)
    pad_end = _dot_hi(padded, (er <= ec).astype(F32))
    pad_start = pad_end - padded
    meta = jnp.where(lax.broadcasted_iota(jnp.int32, (8, LANES), 0) == 0, pad_start, pad_end)
    meta_o[...] = meta.astype(jnp.int32)
    brow = lax.broadcasted_iota(jnp.int32, (n_blocks, LANES), 0).astype(F32) * MOE_ROWS
    done = jnp.where((lane < N_EXPERTS) & (pad_end <= brow), 1.0, 0.0)
    be = jnp.minimum(jnp.sum(done, axis=1, keepdims=True), N_EXPERTS - 1.0)
    blk_o[...] = jnp.broadcast_to(be, (n_blocks, LANES)).astype(jnp.int32)


def _router(h, wr, br, n_blocks, tm=512):
    n, d = h.shape
    kern = functools.partial(_router_kernel, n_blocks=n_blocks)
    tok = pl.BlockSpec((tm, LANES), lambda i: (i, 0))
    return pl.pallas_call(
        kern,
        grid=(n // tm,),
        in_specs=[pl.BlockSpec((tm, d), lambda i: (i, 0)),
                  pl.BlockSpec(wr.shape, lambda i: (0, 0)),
                  pl.BlockSpec((1, LANES), lambda i: (0, 0))],
        out_specs=[tok, tok, tok,
                   pl.BlockSpec((8, LANES), lambda i: (0, 0)),
                   pl.BlockSpec((n_blocks, LANES), lambda i: (0, 0))],
        out_shape=[jax.ShapeDtypeStruct((n, LANES), jnp.int32),
                   jax.ShapeDtypeStruct((n, LANES), F32),
                   jax.ShapeDtypeStruct((n, LANES), jnp.int32),
                   jax.ShapeDtypeStruct((8, LANES), jnp.int32),
                   jax.ShapeDtypeStruct((n_blocks, LANES), jnp.int32)],
        scratch_shapes=[pltpu.VMEM((8, LANES), F32)],
        compiler_params=_cparams(("arbitrary",)),
    )(h, wr, br)


def _dispatch_kernel(idx_ref, rank_ref, start_ref, x_ref, buf_in, buf_ref, sem):
    del buf_in
    tm = x_ref.shape[0]

    def copies(n):
        out = []
        for j in range(TOP_K):
            i = n * TOP_K + j
            dest = start_ref[idx_ref[i]] + rank_ref[i]
            out.append(pltpu.make_async_copy(x_ref.at[pl.ds(n, 1), :],
                                             buf_ref.at[pl.ds(dest, 1), :], sem))
        return out

    def issue(n, c):
        for cp in copies(n):
            cp.start()
        return c

    def drain(n, c):
        for cp in copies(n):
            cp.wait()
        return c

    lax.fori_loop(0, tm, issue, 0, unroll=DMA_UNROLL)
    lax.fori_loop(0, tm, drain, 0, unroll=DMA_UNROLL)


def _dispatch(h, idx_flat, rank_flat, pad_start, n_rows, tm=256):
    n, d = h.shape
    flat = pl.BlockSpec((tm * TOP_K,), lambda i: (i,), memory_space=pltpu.SMEM)
    return pl.pallas_call(
        _dispatch_kernel,
        grid=(n // tm,),
        in_specs=[flat, flat,
                  pl.BlockSpec(memory_space=pltpu.SMEM),
                  pl.BlockSpec((tm, d), lambda i: (i, 0)),
                  pl.BlockSpec(memory_space=pl.ANY)],
        out_specs=pl.BlockSpec(memory_space=pl.ANY),
        out_shape=jax.ShapeDtypeStruct((n_rows, d), F32),
        scratch_shapes=[pltpu.SemaphoreType.DMA(())],
        input_output_aliases={4: 0},
        compiler_params=_cparams(("arbitrary",)),
    )(idx_flat, rank_flat, pad_start, h, jnp.zeros((n_rows, d), F32))


def _expert_kernel(be_ref, nb_ref, x_ref, w1_ref, b1_ref, w2_ref, b2_ref, y_ref, w1_s, w2_s):
    i = pl.program_id(0)
    used = i < nb_ref[0]
    fresh = jnp.logical_or(i == 0, be_ref[i] != be_ref[jnp.maximum(i - 1, 0)])

    @pl.when(jnp.logical_not(used))
    def _():
        y_ref[...] = jnp.zeros_like(y_ref)

    @pl.when(jnp.logical_and(used, fresh))
    def _():
        w1_s[...] = w1_ref[0, 0].astype(BF16)
        w2_s[...] = w2_ref[0, 0].astype(BF16)

    @pl.when(used)
    def _():
        hu = _dot(x_ref[...].astype(BF16), w1_s[...]) + b1_ref[0, 0]
        gate = jnp.minimum(hu[:, :D_FF], SWIGLU_LIMIT)
        up = jnp.clip(hu[:, D_FF:], -SWIGLU_LIMIT, SWIGLU_LIMIT)
        glu = gate * _sigmoid(gate * SWIGLU_ALPHA)
        y_ref[...] = _dot(((up + 1.0) * glu).astype(BF16), w2_s[...]) + b2_ref[0, 0]


def _experts(buf, block_e, n_used, w1, b1, w2, b2, l):
    n_rows, d = buf.shape
    n_blocks = n_rows // MOE_ROWS
    rows = lambda i, be, nb: (jnp.minimum(i, nb[0] - 1), 0)
    wsel = lambda i, be, nb: (l, be[jnp.minimum(i, nb[0] - 1)], 0, 0)
    return pl.pallas_call(
        _expert_kernel,
        grid_spec=pltpu.PrefetchScalarGridSpec(
            num_scalar_prefetch=2,
            grid=(n_blocks,),
            in_specs=[pl.BlockSpec((MOE_ROWS, d), rows),
                      pl.BlockSpec((1, 1, d, 2 * D_FF), wsel),
                      pl.BlockSpec((1, 1, 1, 2 * D_FF), wsel),
                      pl.BlockSpec((1, 1, D_FF, d), wsel),
                      pl.BlockSpec((1, 1, 1, d), wsel)],
            out_specs=pl.BlockSpec((MOE_ROWS, d), lambda i, be, nb: (i, 0)),
            scratch_shapes=[pltpu.VMEM((d, 2 * D_FF), BF16), pltpu.VMEM((D_FF, d), BF16)]),
        out_shape=jax.ShapeDtypeStruct((n_rows, d), F32),
        compiler_params=_cparams(("arbitrary",)),
    )(block_e, n_used, buf, w1, b1[:, :, None, :], w2, b2[:, :, None, :])


def _combine_kernel(idx_ref, rank_ref, start_ref, gate_ref, h_ref, g_ref, b_ref, y_hbm, o_ref,
                    gbuf, sem):
    tm = h_ref.shape[0]

    def copies(n):
        out = []
        for j in range(TOP_K):
            i = n * TOP_K + j
            src = start_ref[idx_ref[i]] + rank_ref[i]
            out.append(pltpu.make_async_copy(y_hbm.at[pl.ds(src, 1), :],
                                             gbuf.at[j, pl.ds(n, 1), :], sem))
        return out

    def issue(n, c):
        for cp in copies(n):
            cp.start()
        return c

    def drain(n, c):
        for cp in copies(n):
            cp.wait()
        return c

    lax.fori_loop(0, tm, issue, 0, unroll=DMA_UNROLL)
    lax.fori_loop(0, tm, drain, 0, unroll=DMA_UNROLL)
    moe = gate_ref[:, 0:1] * gbuf[0]
    for j in range(1, TOP_K):
        moe = moe + gate_ref[:, j:j + 1] * gbuf[j]
    o_ref[...] = _layer_norm(DN_ALPHA * h_ref[...] + moe, g_ref[...], b_ref[...])


def _combine(h, y_buf, idx_flat, rank_flat, pad_start, gates, g, b, tm=256):
    n, d = h.shape
    flat = pl.BlockSpec((tm * TOP_K,), lambda i: (i,), memory_space=pltpu.SMEM)
    return pl.pallas_call(
        _combine_kernel,
        grid=(n // tm,),
        in_specs=[flat, flat,
                  pl.BlockSpec(memory_space=pltpu.SMEM),
                  pl.BlockSpec((tm, LANES), lambda i: (i, 0)),
                  pl.BlockSpec((tm, d), lambda i: (i, 0)),
                  pl.BlockSpec((1, d), lambda i: (0, 0)),
                  pl.BlockSpec((1, d), lambda i: (0, 0)),
                  pl.BlockSpec(memory_space=pl.ANY)],
        out_specs=pl.BlockSpec((tm, d), lambda i: (i, 0)),
        out_shape=jax.ShapeDtypeStruct((n, d), F32),
        scratch_shapes=[pltpu.VMEM((TOP_K, tm, d), F32), pltpu.SemaphoreType.DMA(())],
        compiler_params=_cparams(("arbitrary",)),
    )(idx_flat, rank_flat, pad_start, gates, h, g.reshape(1, d), b.reshape(1, d), y_buf)


def _moe(h, wr, br, w1, b1, w2, b2, g, b, l):
    n, d = h.shape
    n_blocks = n * TOP_K // MOE_ROWS + N_EXPERTS
    wr_pad = jnp.pad(wr, ((0, 0), (0, LANES - N_EXPERTS)))
    br_pad = jnp.pad(br, (0, LANES - N_EXPERTS)).reshape(1, LANES)
    idx, gates, rank, meta, blk = _router(h, wr_pad, br_pad, n_blocks)
    idx_flat = idx[:, :TOP_K].reshape(-1)
    rank_flat = rank[:, :TOP_K].reshape(-1)
    pad_start = meta[0, :N_EXPERTS]
    n_used = meta[1, N_EXPERTS - 1:N_EXPERTS] // MOE_ROWS
    buf = _dispatch(h, idx_flat, rank_flat, pad_start, n_blocks * MOE_ROWS)
    y_buf = _experts(buf, blk[:, 0], n_used, w1, b1, w2, b2, l)
    return _combine(h, y_buf, idx_flat, rank_flat, pad_start, gates, g, b)


def _block_diag_ones(width, head):
    r = jnp.arange(width) // head
    return (r[:, None] == r[None, :]).astype(BF16)


def _mixer_layer(h, bsz, s, p, l, debug=False):
    m_cols = 4 * M_WIDTH + 2 * M_HEADS
    rw = 3 * R_WIDTH
    row = lambda a: a.reshape(1, -1)
    n = bsz * s
    wl = p['w_in'][l]
    w_m = jnp.pad(wl[:, :m_cols], ((0, 0), (0, M_COLS_PAD - m_cols))).astype(BF16)
    perm = lambda a: jnp.concatenate([a[..., :rw], a[..., rw + 128:], a[..., rw:rw + 128]], axis=-1)
    w_r = perm(wl[:, m_cols:]).astype(BF16)
    proj_m, proj_r = _proj(h, w_m, w_r)

    gate_bias = jnp.pad(jnp.concatenate([p['m_ig_b'][l], p['m_fg_b'][l]]),
                        (0, LANES - 2 * M_HEADS))
    hm = _mlstm(proj_m.reshape(bsz, s, M_COLS_PAD), p['m_conv_w'][l], row(p['m_conv_b'][l]),
                row(gate_bias), row(p['m_norm_g'][l]))

    zeros = jnp.zeros((R_WIDTH // 8, R_WIDTH), F32)
    w_lr = jnp.concatenate([jnp.concatenate([p['r_w2'][l], zeros], axis=1),
                            jnp.concatenate([zeros, p['r_a2'][l]], axis=1)], axis=0).astype(BF16)
    parts = _rwkv_prep(proj_r.reshape(bsz, s, R_COLS), row(perm(p['r_mu'][l])), row(p['r_w0'][l]),
                       row(p['r_a0'][l]), w_lr, p['r_g2'][l].astype(BF16), row(p['r_kk'][l]),
                       row(p['r_ka'][l]), _block_diag_ones(R_WIDTH, R_HD))
    hr = _rwkv(parts, row(p['r_rk'][l]), row(p['r_gn_g'][l]), row(p['r_gn_b'][l]))

    h1 = _mix_out(hm.reshape(n, M_WIDTH), hr.reshape(n, R_WIDTH), h, p['w_out'][l].astype(BF16),
                  p['ln1_g'][l], p['ln1_b'][l])
    if debug:
        return h1, hm, hr
    return h1


def _xattn_layer(h, mem2, bsz, s, p, l):
    n, d = h.shape
    n_mem = mem2.shape[0] // bsz
    kv = _mm(mem2, p['x_wkv'][l].astype(BF16), BF16)
    return _xattn(h.reshape(bsz, s, d), kv.reshape(bsz, n_mem, 2 * d), p['x_wq'][l].astype(BF16),
                  p['x_wo'][l].astype(BF16), p['ln2_g'][l], p['ln2_b'][l]).reshape(n, d)


def kernel(x, mem, ln0_g, ln0_b, w_in, m_conv_w, m_conv_b, m_ig_b, m_fg_b, m_norm_g, r_mu, r_w0,
           r_w2, r_a0, r_a2, r_g2, r_kk, r_ka, r_rk, r_gn_g, r_gn_b, w_out, ln1_g, ln1_b, x_wq,
           x_wkv, x_wo, ln2_g, ln2_b, moe_wr, moe_br, moe_w1, moe_b1, moe_w2, moe_b2, ln3_g,
           ln3_b):
    p = dict(w_in=w_in, m_conv_w=m_conv_w, m_conv_b=m_conv_b, m_ig_b=m_ig_b, m_fg_b=m_fg_b,
             m_norm_g=m_norm_g, r_mu=r_mu, r_w0=r_w0, r_w2=r_w2, r_a0=r_a0, r_a2=r_a2, r_g2=r_g2,
             r_kk=r_kk, r_ka=r_ka, r_rk=r_rk, r_gn_g=r_gn_g, r_gn_b=r_gn_b, w_out=w_out,
             ln1_g=ln1_g, ln1_b=ln1_b, x_wq=x_wq, x_wkv=x_wkv, x_wo=x_wo, ln2_g=ln2_g,
             ln2_b=ln2_b)
    bsz, s, d = x.shape
    n = bsz * s
    h = _ln(x.reshape(n, d), ln0_g, ln0_b)
    mem2 = mem.reshape(bsz * mem.shape[1], d)
    for l in range(DEPTH):
        h = _mixer_layer(h, bsz, s, p, l)
        h = _xattn_layer(h, mem2, bsz, s, p, l)
        h = _moe(h, moe_wr[l], moe_br[l], moe_w1, moe_b1, moe_w2, moe_b2, ln3_g[l], ln3_b[l], l)
    return h.reshape(bsz, s, d)
```

```python
import functools

import jax
import jax.numpy as jnp
from jax import lax
from jax.experimental import pallas as pl
from jax.experimental.pallas import tpu as pltpu

F32 = jnp.float32
BF16 = jnp.bfloat16
HIGHEST = lax.Precision.HIGHEST

D_MODEL = 1024
DEPTH = 2
CHUNK = 64
M_WIDTH = 512
M_HEADS = 4
M_HD = 128
M_CONV = 4
R_WIDTH = 512
R_HD = 64
R_HEADS = 8
R_LR = 256
X_HEADS = 4
X_HD = 256
N_EXPERTS = 32
TOP_K = 4
D_FF = 1024
SWIGLU_LIMIT = 7.0
SWIGLU_ALPHA = 1.702
LN_EPS = 1e-5
HEAD_NORM_EPS = 1e-5
RWKV_GN_EPS = 64e-5
DN_ALPHA = (2 * DEPTH) ** 0.25

LANES = 128
M_COLS_PAD = 4 * M_WIDTH + LANES
R_COLS = 3 * R_WIDTH + R_LR
R_GROUP = 256
DMA_UNROLL = 8
MOE_ROWS = 512
VMEM_LIMIT = 56 * 1024 * 1024


def _cparams(sem, vmem=VMEM_LIMIT):
    return pltpu.CompilerParams(dimension_semantics=sem, vmem_limit_bytes=vmem)


def _dot(a, b):
    return jnp.dot(a, b, preferred_element_type=F32)


def _dot_nt(a, b):
    return lax.dot_general(a, b, (((1,), (1,)), ((), ())), preferred_element_type=F32)


def _dot_hi(a, b):
    return jnp.dot(a, b, preferred_element_type=F32, precision=HIGHEST)


def _dot_nt_hi(a, b):
    return lax.dot_general(a, b, (((1,), (1,)), ((), ())), preferred_element_type=F32,
                           precision=HIGHEST)


def _layer_norm(x, g, b):
    mu = jnp.mean(x, axis=-1, keepdims=True)
    xc = x - mu
    var = jnp.mean(xc * xc, axis=-1, keepdims=True)
    return xc * lax.rsqrt(var + LN_EPS) * g + b


def _sigmoid(x):
    return 1.0 / (1.0 + jnp.exp(-x))


def _log_sigmoid(x):
    return jnp.minimum(x, 0.0) - jnp.log1p(jnp.exp(-jnp.abs(x)))


def _ln_kernel(x_ref, g_ref, b_ref, o_ref):
    o_ref[...] = _layer_norm(x_ref[...], g_ref[...], b_ref[...])


def _ln(x, g, b, tm=512):
    n, d = x.shape
    return pl.pallas_call(
        _ln_kernel,
        grid=(n // tm,),
        in_specs=[pl.BlockSpec((tm, d), lambda i: (i, 0)),
                  pl.BlockSpec((1, d), lambda i: (0, 0)),
                  pl.BlockSpec((1, d), lambda i: (0, 0))],
        out_specs=pl.BlockSpec((tm, d), lambda i: (i, 0)),
        out_shape=jax.ShapeDtypeStruct((n, d), F32),
        compiler_params=_cparams(("parallel",)),
    )(x, g.reshape(1, d), b.reshape(1, d))


def _proj_kernel(x_ref, wm_ref, wr_ref, om_ref, or_ref):
    x = x_ref[...].astype(BF16)
    om_ref[...] = _dot(x, wm_ref[...])
    or_ref[...] = _dot(x, wr_ref[...])


def _proj(h, wm, wr, tm=512):
    n, d = h.shape
    return pl.pallas_call(
        _proj_kernel,
        grid=(n // tm,),
        in_specs=[pl.BlockSpec((tm, d), lambda i: (i, 0)),
                  pl.BlockSpec(wm.shape, lambda i: (0, 0)),
                  pl.BlockSpec(wr.shape, lambda i: (0, 0))],
        out_specs=[pl.BlockSpec((tm, wm.shape[1]), lambda i: (i, 0)),
                   pl.BlockSpec((tm, wr.shape[1]), lambda i: (i, 0))],
        out_shape=[jax.ShapeDtypeStruct((n, wm.shape[1]), F32),
                   jax.ShapeDtypeStruct((n, wr.shape[1]), F32)],
        compiler_params=_cparams(("parallel",)),
    )(h, wm, wr)


def _mm_kernel(x_ref, w_ref, o_ref):
    o_ref[...] = _dot(x_ref[...].astype(BF16), w_ref[...]).astype(o_ref.dtype)


def _mm(x, w, out_dtype, tm=512):
    n, d = x.shape
    return pl.pallas_call(
        _mm_kernel,
        grid=(n // tm,),
        in_specs=[pl.BlockSpec((tm, d), lambda i: (i, 0)),
                  pl.BlockSpec(w.shape, lambda i: (0, 0))],
        out_specs=pl.BlockSpec((tm, w.shape[1]), lambda i: (i, 0)),
        out_shape=jax.ShapeDtypeStruct((n, w.shape[1]), out_dtype),
        compiler_params=_cparams(("parallel",)),
    )(x, w)


def _mix_out_kernel(hm_ref, hr_ref, h_ref, w_ref, g_ref, b_ref, o_ref):
    mix = _dot(hm_ref[...].astype(BF16), w_ref[:M_WIDTH, :])
    mix = mix + _dot(hr_ref[...].astype(BF16), w_ref[M_WIDTH:, :])
    o_ref[...] = _layer_norm(DN_ALPHA * h_ref[...] + mix, g_ref[...], b_ref[...])


def _mix_out(hm, hr, h, w, g, b, tm=512):
    n, d = h.shape
    return pl.pallas_call(
        _mix_out_kernel,
        grid=(n // tm,),
        in_specs=[pl.BlockSpec((tm, M_WIDTH), lambda i: (i, 0)),
                  pl.BlockSpec((tm, R_WIDTH), lambda i: (i, 0)),
                  pl.BlockSpec((tm, d), lambda i: (i, 0)),
                  pl.BlockSpec(w.shape, lambda i: (0, 0)),
                  pl.BlockSpec((1, d), lambda i: (0, 0)),
                  pl.BlockSpec((1, d), lambda i: (0, 0))],
        out_specs=pl.BlockSpec((tm, d), lambda i: (i, 0)),
        out_shape=jax.ShapeDtypeStruct((n, d), F32),
        compiler_params=_cparams(("parallel",)),
    )(hm, hr, h, w, g.reshape(1, d), b.reshape(1, d))


def _mlstm_kernel(qk_ref, v_ref, o_ref, gt_ref, cw_ref, cb_ref, gb_ref, ng_ref, out_ref,
                  ct_s, n_s, m_s, tail_s, *, n_chunks):
    L = CHUNK

    @pl.when(pl.program_id(1) == 0)
    def _():
        ct_s[...] = jnp.zeros_like(ct_s)
        n_s[...] = jnp.zeros_like(n_s)
        m_s[...] = jnp.zeros_like(m_s)
        tail_s[...] = jnp.zeros_like(tail_s)

    row = lax.broadcasted_iota(jnp.int32, (L, L), 0)
    col = lax.broadcasted_iota(jnp.int32, (L, L), 1)
    causal = row >= col
    tril_f = causal.astype(F32)
    lane = lax.broadcasted_iota(jnp.int32, (1, LANES), 1)

    def chunk(c, carry):
        r0 = pl.multiple_of(c * L, L)
        x = qk_ref[0, pl.ds(r0, L), :]
        cat = jnp.concatenate([tail_s[...], x], axis=0)
        acc = cb_ref[...] + cw_ref[M_CONV - 1:M_CONV, :] * x
        for j in range(M_CONV - 1):
            acc = acc + cw_ref[j:j + 1, :] * pltpu.roll(cat, M_CONV - 1 - j, axis=0)[8:, :]
        tail_s[...] = x[L - 8:, :]
        qk = acc * _sigmoid(acc)

        gx = gt_ref[0, pl.ds(r0, L), :] + gb_ref[...]
        bc = _dot_hi(tril_f, _log_sigmoid(gx))
        rows = jnp.where(lane < M_HEADS, gx, bc).T

        for h in range(M_HEADS):
            q = qk[:, h * M_HD:(h + 1) * M_HD]
            k = qk[:, M_WIDTH + h * M_HD:M_WIDTH + (h + 1) * M_HD] * (M_HD ** -0.5)
            v = v_ref[0, pl.ds(r0, L), h * M_HD:(h + 1) * M_HD]
            ig_c = gx[:, h:h + 1]
            b_c = bc[:, M_HEADS + h:M_HEADS + h + 1]
            ig_r = rows[h:h + 1, :]
            b_r = rows[M_HEADS + h:M_HEADS + h + 1, :]
            b_last = bc[L - 1:L, M_HEADS + h:M_HEADS + h + 1]
            m_prev = m_s[h, 0:1, 0:1]
            n_prev = n_s[h, 0:1, :]
            ct_prev = ct_s[h]

            qb = q.astype(BF16)
            kb = k.astype(BF16)
            vb = v.astype(BF16)

            g_inter = b_c + m_prev
            d_mat = jnp.where(causal, b_c - b_r + ig_r, -jnp.inf)
            m_t = jnp.maximum(g_inter, jnp.max(d_mat, axis=1, keepdims=True))
            s = _dot_nt(qb, kb) * jnp.exp(d_mat - m_t)
            w_inter = jnp.exp(g_inter - m_t)
            num = _dot(s.astype(BF16), vb) + w_inter * _dot(qb, ct_prev.astype(BF16))
            den = (jnp.sum(s, axis=1, keepdims=True)
                   + w_inter * jnp.sum(q * n_prev, axis=1, keepdims=True))
            hc = num / jnp.maximum(jnp.abs(den), jnp.exp(-m_t))
            mu = jnp.mean(hc, axis=1, keepdims=True)
            hcc = hc - mu
            var = jnp.mean(hcc * hcc, axis=1, keepdims=True)
            og = o_ref[0, pl.ds(r0, L), h * M_HD:(h + 1) * M_HD]
            out = hcc * lax.rsqrt(var + HEAD_NORM_EPS) * ng_ref[:, h * M_HD:(h + 1) * M_HD]
            out_ref[0, pl.ds(r0, L), h * M_HD:(h + 1) * M_HD] = out * _sigmoid(og)

            src = b_last - b_c + ig_c
            m_loc = jnp.max(src, axis=0, keepdims=True)
            kw = k * jnp.exp(src - m_loc)
            ct_loc = _dot(kw.T.astype(BF16), vb)
            n_loc = jnp.sum(kw, axis=0, keepdims=True)
            m_new = jnp.maximum(b_last + m_prev, m_loc)
            a = jnp.exp(b_last + m_prev - m_new)
            cc = jnp.exp(m_loc - m_new)
            ct_s[h] = a * ct_prev + cc * ct_loc
            n_s[h] = jnp.broadcast_to(a * n_prev + cc * n_loc, (8, M_HD))
            m_s[h] = jnp.broadcast_to(m_new, (8, LANES))
        return carry

    lax.fori_loop(0, n_chunks, chunk, 0)


def _mlstm(proj_m, cw, cb, gb, ng, ts=512):
    bsz, s, _ = proj_m.shape
    kern = functools.partial(_mlstm_kernel, n_chunks=ts // CHUNK)
    return pl.pallas_call(
        kern,
        grid=(bsz, s // ts),
        in_specs=[pl.BlockSpec((1, ts, 2 * M_WIDTH), lambda b, t: (b, t, 0)),
                  pl.BlockSpec((1, ts, M_WIDTH), lambda b, t: (b, t, 2)),
                  pl.BlockSpec((1, ts, M_WIDTH), lambda b, t: (b, t, 3)),
                  pl.BlockSpec((1, ts, LANES), lambda b, t: (b, t, 4 * M_WIDTH // LANES)),
                  pl.BlockSpec((M_CONV, 2 * M_WIDTH), lambda b, t: (0, 0)),
                  pl.BlockSpec((1, 2 * M_WIDTH), lambda b, t: (0, 0)),
                  pl.BlockSpec((1, LANES), lambda b, t: (0, 0)),
                  pl.BlockSpec((1, M_WIDTH), lambda b, t: (0, 0))],
        out_specs=pl.BlockSpec((1, ts, M_WIDTH), lambda b, t: (b, t, 0)),
        out_shape=jax.ShapeDtypeStruct((bsz, s, M_WIDTH), F32),
        scratch_shapes=[pltpu.VMEM((M_HEADS, M_HD, M_HD), F32),
                        pltpu.VMEM((M_HEADS, 8, M_HD), F32),
                        pltpu.VMEM((M_HEADS, 8, LANES), F32),
                        pltpu.VMEM((8, 2 * M_WIDTH), F32)],
        compiler_params=_cparams(("parallel", "arbitrary")),
    )(proj_m, proj_m, proj_m, proj_m, cw, cb, gb, ng)


def _rwkv_prep_kernel(p_ref, pp_ref, mu_ref, w0_ref, a0_ref, wlr_ref, g2_ref, kkp_ref, ka_ref,
                      bd_ref, r_o, k_o, v_o, kk_o, a_o, ld_o, g_o):
    ts = p_ref.shape[1]
    p = p_ref[0]
    first = pl.program_id(1) == 0
    prev_last = jnp.where(first, 0.0, pp_ref[0, 7:8, :])
    row = lax.broadcasted_iota(jnp.int32, (ts, 1), 0)
    prev = jnp.where(row == 0, prev_last, pltpu.roll(p, 1, axis=0))
    xs = p + (prev - p) * mu_ref[...]
    rr = xs[:, :R_WIDTH]
    rk = xs[:, R_WIDTH:2 * R_WIDTH]
    rv = xs[:, 2 * R_WIDTH:3 * R_WIDTH]
    gl = xs[:, 3 * R_WIDTH:3 * R_WIDTH + LANES]
    wa = xs[:, 3 * R_WIDTH + LANES:]
    lane = lax.broadcasted_iota(jnp.int32, (1, LANES), 1)
    wa = jnp.where(lane < LANES // 2, jnp.tanh(wa), wa)
    wa_out = _dot(wa.astype(BF16), wlr_ref[...])
    w_log = _log_sigmoid(w0_ref[...] + wa_out[:, :R_WIDTH]) - 0.5
    a = _sigmoid(a0_ref[...] + wa_out[:, R_WIDTH:])
    kk = rk * kkp_ref[...]
    sq = kk * kk
    sq_hi = sq.astype(BF16)
    sq_lo = (sq - sq_hi.astype(F32)).astype(BF16)
    ss = _dot(sq_hi, bd_ref[...]) + _dot(sq_lo, bd_ref[...])
    r_o[0] = rr
    k_o[0] = rk * (1.0 + (a - 1.0) * ka_ref[...])
    v_o[0] = rv
    kk_o[0] = kk / jnp.maximum(jnp.sqrt(ss), 1e-12)
    a_o[0] = a
    ld_o[0] = -jnp.exp(w_log)
    g_o[0] = _dot(_sigmoid(gl).astype(BF16), g2_ref[...])


def _rwkv_prep(proj_r, mu, w0, a0, wlr, g2, kkp, ka, bd, ts=256):
    bsz, s, _ = proj_r.shape
    vec = lambda w: pl.BlockSpec((1, w), lambda b, t: (0, 0))
    full = lambda a: pl.BlockSpec(a.shape, lambda b, t: (0, 0))
    out_spec = pl.BlockSpec((1, ts, R_WIDTH), lambda b, t: (b, t, 0))
    out_sds = jax.ShapeDtypeStruct((bsz, s, R_WIDTH), F32)
    return pl.pallas_call(
        _rwkv_prep_kernel,
        grid=(bsz, s // ts),
        in_specs=[pl.BlockSpec((1, ts, R_COLS), lambda b, t: (b, t, 0)),
                  pl.BlockSpec((1, 8, R_COLS),
                               lambda b, t: (b, jnp.maximum(t * (ts // 8) - 1, 0), 0)),
                  vec(R_COLS), vec(R_WIDTH), vec(R_WIDTH), full(wlr), full(g2),
                  vec(R_WIDTH), vec(R_WIDTH), full(bd)],
        out_specs=[out_spec] * 7,
        out_shape=[out_sds] * 7,
        compiler_params=_cparams(("parallel", "arbitrary")),
    )(proj_r, proj_r, mu, w0, a0, wlr, g2, kkp, ka, bd)


def _group_head_sum(x, masks):
    out = jnp.sum(jnp.where(masks[-1], x, 0.0), axis=1, keepdims=True)
    for m in masks[-2::-1]:
        out = jnp.where(m, jnp.sum(jnp.where(m, x, 0.0), axis=1, keepdims=True), out)
    return out


def _rwkv_intra_kernel(r_ref, k_ref, v_ref, kk_ref, a_ref, ld_ref, g_ref, rkp_ref, gng_ref,
                       gnb_ref, ah_o, rq_o, be_o, kt_o, uh_o, y0_o, e1_o, e2_o, dl_o, *, n_chunks):
    L = CHUNK
    G = R_GROUP
    row = lax.broadcasted_iota(jnp.int32, (L, L), 0)
    col = lax.broadcasted_iota(jnp.int32, (L, L), 1)
    tril_b = (row >= col).astype(BF16)
    br = lax.broadcasted_iota(jnp.int32, (G, G), 0)
    bc = lax.broadcasted_iota(jnp.int32, (G, G), 1)
    same = (br // L) == (bc // L)
    lower_s = same & ((br % L) > (bc % L))
    lower_i = same & ((br % L) >= (bc % L))
    eye = (br == bc).astype(F32)
    lane = lax.broadcasted_iota(jnp.int32, (1, G), 1)
    masks = [(lane // R_HD) == h for h in range(G // R_HD)]

    def stack(x):
        return jnp.concatenate([jnp.where(m, x, 0.0) for m in masks], axis=0).astype(BF16)

    def collapse(x):
        out = x[0:L]
        for h in range(1, G // R_HD):
            out = out + x[h * L:(h + 1) * L]
        return out

    for ci in range(n_chunks):
        rows = slice(ci * L, (ci + 1) * L)
        for gi in range(R_WIDTH // G):
            sl = slice(gi * G, (gi + 1) * G)
            r = r_ref[0, rows, sl]
            k = k_ref[0, rows, sl]
            v = v_ref[0, rows, sl]
            kk = kk_ref[0, rows, sl]
            a = a_ref[0, rows, sl]
            ld = ld_ref[0, rows, sl]

            ld_hi = ld.astype(BF16)
            ld_lo = (ld - ld_hi.astype(F32)).astype(BF16)
            cum = _dot(tril_b, ld_hi) + _dot(tril_b, ld_lo)
            dec = jnp.exp(cum)
            inv = jnp.exp(-cum)
            al = -kk * jnp.exp(cum - ld)
            be = kk * a * inv
            kt = k * inv
            rt = r * dec
            dl = dec[L - 1:L, :]

            al4 = stack(al)
            v4 = stack(v)
            big = _dot_nt(jnp.concatenate([al4, stack(rt)], axis=0),
                          jnp.concatenate([stack(be), stack(kt)], axis=0))
            ab = jnp.where(lower_s, big[:G, :G], 0.0)
            ak = jnp.where(lower_s, big[:G, G:], 0.0)
            rb = jnp.where(lower_i, big[G:, :G], 0.0).astype(BF16)
            rk = jnp.where(lower_i, big[G:, G:], 0.0).astype(BF16)
            w = eye + ab
            xp = ab
            for _ in range(5):
                xpb = xp.astype(BF16)
                xp = _dot(xpb, xpb)
                w = w + _dot(w.astype(BF16), xp.astype(BF16))
            wb = w.astype(BF16)
            ahs = _dot(wb, al4)
            uhs = _dot(wb, _dot(ak.astype(BF16), v4).astype(BF16))
            rqs = _dot(rb, ahs.astype(BF16))
            y0s = _dot(rb, uhs.astype(BF16)) + _dot(rk, v4)

            ah_o[0, rows, sl] = collapse(ahs).astype(BF16)
            rq_o[0, rows, sl] = (rt + collapse(rqs)).astype(BF16)
            be_o[0, rows, sl] = (be * dl).astype(BF16)
            kt_o[0, rows, sl] = (kt * dl).astype(BF16)
            uh_o[0, rows, sl] = collapse(uhs)
            y0_o[0, rows, sl] = collapse(y0s)
            g = g_ref[0, rows, sl]
            bonus = _group_head_sum(r * k * rkp_ref[:, sl], masks) * v
            e1_o[0, rows, sl] = g * gng_ref[:, sl]
            e2_o[0, rows, sl] = (gnb_ref[:, sl] + bonus) * g
            dl_o[0, ci * 8:(ci + 1) * 8, sl] = jnp.broadcast_to(dl, (8, G))


def _rwkv_seq_kernel(ah_ref, rq_ref, be_ref, kt_ref, v_ref, uh_ref, y0_ref, e1_ref, e2_ref, dl_ref,
                     out_ref, z_s, *, n_chunks):
    L = CHUNK
    G = R_GROUP

    @pl.when(pl.program_id(1) == 0)
    def _():
        z_s[...] = jnp.zeros_like(z_s)

    zr = lax.broadcasted_iota(jnp.int32, (G, G), 0) // R_HD
    zc = lax.broadcasted_iota(jnp.int32, (G, G), 1) // R_HD
    same_head = zr == zc
    lane = lax.broadcasted_iota(jnp.int32, (1, G), 1)
    masks = [(lane // R_HD) == h for h in range(G // R_HD)]

    def chunk(c, carry):
        r0 = pl.multiple_of(c * L, L)
        d0 = pl.multiple_of(c * 8, 8)
        for gi in range(R_WIDTH // G):
            sl = slice(gi * G, (gi + 1) * G)
            zt = z_s[gi]
            lhs = jnp.concatenate([ah_ref[0, pl.ds(r0, L), sl], rq_ref[0, pl.ds(r0, L), sl]],
                                  axis=0)
            pr = _dot_nt(lhs, zt.astype(BF16))
            u = pr[:L] + uh_ref[0, pl.ds(r0, L), sl]
            y = pr[L:] + y0_ref[0, pl.ds(r0, L), sl]
            uv = jnp.concatenate([u, v_ref[0, pl.ds(r0, L), sl]], axis=0)
            bk = jnp.concatenate([be_ref[0, pl.ds(r0, L), sl], kt_ref[0, pl.ds(r0, L), sl]],
                                 axis=0)
            upd = _dot(uv.T.astype(BF16), bk)
            z_s[gi] = zt * dl_ref[0, pl.ds(d0, 1), sl] + jnp.where(same_head, upd, 0.0)

            mu = _group_head_sum(y, masks) * (1.0 / R_HD)
            yc = y - mu
            var = _group_head_sum(yc * yc, masks) * (1.0 / R_HD)
            out_ref[0, pl.ds(r0, L), sl] = (yc * lax.rsqrt(var + RWKV_GN_EPS)
                                            * e1_ref[0, pl.ds(r0, L), sl]
                                            + e2_ref[0, pl.ds(r0, L), sl])
        return carry

    lax.fori_loop(0, n_chunks, chunk, 0)


def _rwkv(parts, rkp, gng, gnb, ts_intra=128, ts_seq=512):
    bsz, s, _ = parts[0].shape
    vec = pl.BlockSpec((1, R_WIDTH), lambda b, t: (0, 0))
    blk = pl.BlockSpec((1, ts_intra, R_WIDTH), lambda b, t: (b, t, 0))
    sds = lambda dt: jax.ShapeDtypeStruct((bsz, s, R_WIDTH), dt)
    ah, rq, be, kt, uh, y0, e1, e2, dl = pl.pallas_call(
        functools.partial(_rwkv_intra_kernel, n_chunks=ts_intra // CHUNK),
        grid=(bsz, s // ts_intra),
        in_specs=[blk] * 7 + [vec] * 3,
        out_specs=[blk] * 8 + [pl.BlockSpec((1, ts_intra // 8, R_WIDTH), lambda b, t: (b, t, 0))],
        out_shape=[sds(BF16)] * 4 + [sds(F32)] * 4
        + [jax.ShapeDtypeStruct((bsz, s // 8, R_WIDTH), F32)],
        compiler_params=_cparams(("parallel", "parallel")),
    )(*parts, rkp, gng, gnb)
    blk = pl.BlockSpec((1, ts_seq, R_WIDTH), lambda b, t: (b, t, 0))
    return pl.pallas_call(
        functools.partial(_rwkv_seq_kernel, n_chunks=ts_seq // CHUNK),
        grid=(bsz, s // ts_seq),
        in_specs=[blk] * 9 + [pl.BlockSpec((1, ts_seq // 8, R_WIDTH), lambda b, t: (b, t, 0))],
        out_specs=blk,
        out_shape=sds(F32),
        scratch_shapes=[pltpu.VMEM((R_WIDTH // R_GROUP, R_GROUP, R_GROUP), F32)],
        compiler_params=_cparams(("parallel", "arbitrary")),
    )(ah, rq, be, kt, parts[2], uh, y0, e1, e2, dl)


def _xattn_kernel(h_ref, kv_ref, wq_ref, wo_ref, g_ref, b_ref, o_ref):
    hq = h_ref[0]
    q = _dot(hq.astype(BF16), wq_ref[...])
    acc = jnp.zeros_like(hq)
    for hd in range(X_HEADS):
        sl = slice(hd * X_HD, (hd + 1) * X_HD)
        kh = kv_ref[0, :, sl]
        vh = kv_ref[0, :, D_MODEL + hd * X_HD:D_MODEL + (hd + 1) * X_HD]
        s = _dot_nt(q[:, sl].astype(BF16), kh) * (X_HD ** -0.5)
        e = jnp.exp(s - jnp.max(s, axis=1, keepdims=True))
        p = e / jnp.sum(e, axis=1, keepdims=True)
        oh = _dot(p.astype(BF16), vh)
        acc = acc + _dot(oh.astype(BF16), wo_ref[sl, :])
    o_ref[0] = _layer_norm(DN_ALPHA * hq + acc, g_ref[...], b_ref[...])


def _xattn(h3, kv3, wq, wo, g, b, tq=512):
    bsz, s, d = h3.shape
    m = kv3.shape[1]
    return pl.pallas_call(
        _xattn_kernel,
        grid=(bsz, s // tq),
        in_specs=[pl.BlockSpec((1, tq, d), lambda i, t: (i, t, 0)),
                  pl.BlockSpec((1, m, 2 * d), lambda i, t: (i, 0, 0)),
                  pl.BlockSpec(wq.shape, lambda i, t: (0, 0)),
                  pl.BlockSpec(wo.shape, lambda i, t: (0, 0)),
                  pl.BlockSpec((1, d), lambda i, t: (0, 0)),
                  pl.BlockSpec((1, d), lambda i, t: (0, 0))],
        out_specs=pl.BlockSpec((1, tq, d), lambda i, t: (i, t, 0)),
        out_shape=jax.ShapeDtypeStruct((bsz, s, d), F32),
        compiler_params=_cparams(("parallel", "parallel")),
    )(h3, kv3, wq, wo, g.reshape(1, d), b.reshape(1, d))


def _router_kernel(h_ref, wr_ref, br_ref, idx_o, gate_o, rank_o, meta_o, blk_o, cnt_s,
                   *, n_blocks):
    tm = h_ref.shape[0]
    step = pl.program_id(0)

    @pl.when(step == 0)
    def _():
        cnt_s[...] = jnp.zeros_like(cnt_s)

    lane = lax.broadcasted_iota(jnp.int32, (1, LANES), 1)
    logits = _dot_hi(h_ref[...], wr_ref[...]) + br_ref[...]
    logits = jnp.where(lane < N_EXPERTS, logits, -jnp.inf)
    vals, sels = [], []
    idx_out = jnp.zeros((tm, LANES), jnp.int32)
    for j in range(TOP_K):
        mx = jnp.max(logits, axis=1, keepdims=True)
        ij = jnp.min(jnp.where(logits == mx, lane, LANES), axis=1, keepdims=True)
        sel = lane == ij
        vals.append(mx)
        sels.append(sel)
        idx_out = jnp.where(lane == j, ij, idx_out)
        logits = jnp.where(sel, -jnp.inf, logits)
    exps = [jnp.exp(vj - vals[0]) for vj in vals]
    denom = exps[0] + exps[1] + exps[2] + exps[3]
    gate_out = jnp.zeros((tm, LANES), F32)
    for j in range(TOP_K):
        gate_out = jnp.where(lane == j, exps[j] / denom, gate_out)

    cnt = jnp.zeros((tm, LANES), F32)
    for sel in sels:
        cnt = cnt + sel.astype(F32)
    row = lax.broadcasted_iota(jnp.int32, (tm, tm), 0)
    col = lax.broadcasted_iota(jnp.int32, (tm, tm), 1)
    before = (row > col).astype(BF16)
    excl = _dot(before, cnt.astype(BF16)) + cnt_s[0:1, :]
    rank_out = jnp.zeros((tm, LANES), jnp.int32)
    for j in range(TOP_K):
        rj = jnp.sum(jnp.where(sels[j], excl, 0.0), axis=1, keepdims=True)
        rank_out = jnp.where(lane == j, rj.astype(jnp.int32), rank_out)
    total = cnt_s[0:1, :] + jnp.sum(cnt, axis=0, keepdims=True)
    cnt_s[...] = jnp.broadcast_to(total, cnt_s.shape)

    idx_o[...] = idx_out
    gate_o[...] = gate_out
    rank_o[...] = rank_out

    padded = jnp.floor((total + (MOE_ROWS - 1)) * (1.0 / MOE_ROWS)) * MOE_ROWS
    er = lax.broadcasted_iota(jnp.int32, (LANES, LANES), 0)
    ec = lax.broadcasted_iota(jnp.int32, (LANES, LANES), 1)
    pad_end = _dot_hi(padded, (er <= ec).astype(F32))
    pad_start = pad_end - padded
    meta = jnp.where(lax.broadcasted_iota(jnp.int32, (8, LANES), 0) == 0, pad_start, pad_end)
    meta_o[...] = meta.astype(jnp.int32)
    brow = lax.broadcasted_iota(jnp.int32, (n_blocks, LANES), 0).astype(F32) * MOE_ROWS
    done = jnp.where((lane < N_EXPERTS) & (pad_end <= brow), 1.0, 0.0)
    be = jnp.minimum(jnp.sum(done, axis=1, keepdims=True), N_EXPERTS - 1.0)
    blk_o[...] = jnp.broadcast_to(be, (n_blocks, LANES)).astype(jnp.int32)


def _router(h, wr, br, n_blocks, tm=512):
    n, d = h.shape
    kern = functools.partial(_router_kernel, n_blocks=n_blocks)
    tok = pl.BlockSpec((tm, LANES), lambda i: (i, 0))
    return pl.pallas_call(
        kern,
        grid=(n // tm,),
        in_specs=[pl.BlockSpec((tm, d), lambda i: (i, 0)),
                  pl.BlockSpec(wr.shape, lambda i: (0, 0)),
                  pl.BlockSpec((1, LANES), lambda i: (0, 0))],
        out_specs=[tok, tok, tok,
                   pl.BlockSpec((8, LANES), lambda i: (0, 0)),
                   pl.BlockSpec((n_blocks, LANES), lambda i: (0, 0))],
        out_shape=[jax.ShapeDtypeStruct((n, LANES), jnp.int32),
                   jax.ShapeDtypeStruct((n, LANES), F32),
                   jax.ShapeDtypeStruct((n, LANES), jnp.int32),
                   jax.ShapeDtypeStruct((8, LANES), jnp.int32),
                   jax.ShapeDtypeStruct((n_blocks, LANES), jnp.int32)],
        scratch_shapes=[pltpu.VMEM((8, LANES), F32)],
        compiler_params=_cparams(("arbitrary",)),
    )(h, wr, br)


def _dispatch_kernel(idx_ref, rank_ref, start_ref, x_ref, buf_in, buf_ref, sem):
    del buf_in
    tm = x_ref.shape[0]

    def copies(n):
        out = []
        for j in range(TOP_K):
            i = n * TOP_K + j
            dest = start_ref[idx_ref[i]] + rank_ref[i]
            out.append(pltpu.make_async_copy(x_ref.at[pl.ds(n, 1), :],
                                             buf_ref.at[pl.ds(dest, 1), :], sem))
        return out

    def issue(n, c):
        for cp in copies(n):
            cp.start()
        return c

    def drain(n, c):
        for cp in copies(n):
            cp.wait()
        return c

    lax.fori_loop(0, tm, issue, 0, unroll=DMA_UNROLL)
    lax.fori_loop(0, tm, drain, 0, unroll=DMA_UNROLL)


def _dispatch(h, idx_flat, rank_flat, pad_start, n_rows, tm=512):
    n, d = h.shape
    flat = pl.BlockSpec((tm * TOP_K,), lambda i: (i,), memory_space=pltpu.SMEM)
    return pl.pallas_call(
        _dispatch_kernel,
        grid=(n // tm,),
        in_specs=[flat, flat,
                  pl.BlockSpec(memory_space=pltpu.SMEM),
                  pl.BlockSpec((tm, d), lambda i: (i, 0)),
                  pl.BlockSpec(memory_space=pl.ANY)],
        out_specs=pl.BlockSpec(memory_space=pl.ANY),
        out_shape=jax.ShapeDtypeStruct((n_rows, d), F32),
        scratch_shapes=[pltpu.SemaphoreType.DMA(())],
        input_output_aliases={4: 0},
        compiler_params=_cparams(("arbitrary",)),
    )(idx_flat, rank_flat, pad_start, h, jnp.zeros((n_rows, d), F32))


def _expert_kernel(be_ref, nb_ref, x_ref, w1_ref, b1_ref, w2_ref, b2_ref, y_ref, w1_s, w2_s):
    i = pl.program_id(0)
    used = i < nb_ref[0]
    fresh = jnp.logical_or(i == 0, be_ref[i] != be_ref[jnp.maximum(i - 1, 0)])

    @pl.when(jnp.logical_not(used))
    def _():
        y_ref[...] = jnp.zeros_like(y_ref)

    @pl.when(jnp.logical_and(used, fresh))
    def _():
        w1_s[...] = w1_ref[0, 0].astype(BF16)
        w2_s[...] = w2_ref[0, 0].astype(BF16)

    @pl.when(used)
    def _():
        hu = _dot(x_ref[...].astype(BF16), w1_s[...]) + b1_ref[0, 0]
        gate = jnp.minimum(hu[:, :D_FF], SWIGLU_LIMIT)
        up = jnp.clip(hu[:, D_FF:], -SWIGLU_LIMIT, SWIGLU_LIMIT)
        glu = gate * _sigmoid(gate * SWIGLU_ALPHA)
        y_ref[...] = _dot(((up + 1.0) * glu).astype(BF16), w2_s[...]) + b2_ref[0, 0]


def _experts(buf, block_e, n_used, w1, b1, w2, b2, l):
    n_rows, d = buf.shape
    n_blocks = n_rows // MOE_ROWS
    last = lambda i, nb: jnp.maximum(jnp.minimum(i, nb[0] - 1), 0)
    rows = lambda i, be, nb: (last(i, nb), 0)
    wsel = lambda i, be, nb: (l, be[last(i, nb)], 0, 0)
    return pl.pallas_call(
        _expert_kernel,
        grid_spec=pltpu.PrefetchScalarGridSpec(
            num_scalar_prefetch=2,
            grid=(n_blocks,),
            in_specs=[pl.BlockSpec((MOE_ROWS, d), rows),
                      pl.BlockSpec((1, 1, d, 2 * D_FF), wsel),
                      pl.BlockSpec((1, 1, 1, 2 * D_FF), wsel),
                      pl.BlockSpec((1, 1, D_FF, d), wsel),
                      pl.BlockSpec((1, 1, 1, d), wsel)],
            out_specs=pl.BlockSpec((MOE_ROWS, d), lambda i, be, nb: (i, 0)),
            scratch_shapes=[pltpu.VMEM((d, 2 * D_FF), BF16), pltpu.VMEM((D_FF, d), BF16)]),
        out_shape=jax.ShapeDtypeStruct((n_rows, d), F32),
        compiler_params=_cparams(("arbitrary",)),
    )(block_e, n_used, buf, w1, b1[:, :, None, :], w2, b2[:, :, None, :])


def _combine_kernel(idx_ref, rank_ref, start_ref, gate_ref, h_ref, g_ref, b_ref, y_hbm, o_ref,
                    gbuf, sem):
    tm = h_ref.shape[0]

    def copies(n):
        out = []
        for j in range(TOP_K):
            i = n * TOP_K + j
            src = start_ref[idx_ref[i]] + rank_ref[i]
            out.append(pltpu.make_async_copy(y_hbm.at[pl.ds(src, 1), :],
                                             gbuf.at[j, pl.ds(n, 1), :], sem))
        return out

    def issue(n, c):
        for cp in copies(n):
            cp.start()
        return c

    def drain(n, c):
        for cp in copies(n):
            cp.wait()
        return c

    lax.fori_loop(0, tm, issue, 0, unroll=DMA_UNROLL)
    lax.fori_loop(0, tm, drain, 0, unroll=DMA_UNROLL)
    moe = gate_ref[:, 0:1] * gbuf[0]
    for j in range(1, TOP_K):
        moe = moe + gate_ref[:, j:j + 1] * gbuf[j]
    o_ref[...] = _layer_norm(DN_ALPHA * h_ref[...] + moe, g_ref[...], b_ref[...])


def _combine(h, y_buf, idx_flat, rank_flat, pad_start, gates, g, b, tm=512):
    n, d = h.shape
    flat = pl.BlockSpec((tm * TOP_K,), lambda i: (i,), memory_space=pltpu.SMEM)
    return pl.pallas_call(
        _combine_kernel,
        grid=(n // tm,),
        in_specs=[flat, flat,
                  pl.BlockSpec(memory_space=pltpu.SMEM),
                  pl.BlockSpec((tm, LANES), lambda i: (i, 0)),
                  pl.BlockSpec((tm, d), lambda i: (i, 0)),
                  pl.BlockSpec((1, d), lambda i: (0, 0)),
                  pl.BlockSpec((1, d), lambda i: (0, 0)),
                  pl.BlockSpec(memory_space=pl.ANY)],
        out_specs=pl.BlockSpec((tm, d), lambda i: (i, 0)),
        out_shape=jax.ShapeDtypeStruct((n, d), F32),
        scratch_shapes=[pltpu.VMEM((TOP_K, tm, d), F32), pltpu.SemaphoreType.DMA(())],
        compiler_params=_cparams(("arbitrary",)),
    )(idx_flat, rank_flat, pad_start, gates, h, g.reshape(1, d), b.reshape(1, d), y_buf)


def _moe(h, wr, br, w1, b1, w2, b2, g, b, l):
    n, d = h.shape
    n_blocks = n * TOP_K // MOE_ROWS + N_EXPERTS
    wr_pad = jnp.pad(wr, ((0, 0), (0, LANES - N_EXPERTS)))
    br_pad = jnp.pad(br, (0, LANES - N_EXPERTS)).reshape(1, LANES)
    idx, gates, rank, meta, blk = _router(h, wr_pad, br_pad, n_blocks)
    idx_flat = idx[:, :TOP_K].reshape(-1)
    rank_flat = rank[:, :TOP_K].reshape(-1)
    pad_start = meta[0, :N_EXPERTS]
    n_used = meta[1, N_EXPERTS - 1:N_EXPERTS] // MOE_ROWS
    buf = _dispatch(h, idx_flat, rank_flat, pad_start, n_blocks * MOE_ROWS)
    y_buf = _experts(buf, blk[:, 0], n_used, w1, b1, w2, b2, l)
    return _combine(h, y_buf, idx_flat, rank_flat, pad_start, gates, g, b)


def _block_diag_ones(width, head):
    r = jnp.arange(width) // head
    return (r[:, None] == r[None, :]).astype(BF16)


def _mixer_layer(h, bsz, s, p, l, debug=False):
    m_cols = 4 * M_WIDTH + 2 * M_HEADS
    rw = 3 * R_WIDTH
    row = lambda a: a.reshape(1, -1)
    n = bsz * s
    wl = p['w_in'][l]
    w_m = jnp.pad(wl[:, :m_cols], ((0, 0), (0, M_COLS_PAD - m_cols))).astype(BF16)
    perm = lambda a: jnp.concatenate([a[..., :rw], a[..., rw + 128:], a[..., rw:rw + 128]], axis=-1)
    w_r = perm(wl[:, m_cols:]).astype(BF16)
    proj_m, proj_r = _proj(h, w_m, w_r)

    gate_bias = jnp.pad(jnp.concatenate([p['m_ig_b'][l], p['m_fg_b'][l]]),
                        (0, LANES - 2 * M_HEADS))
    hm = _mlstm(proj_m.reshape(bsz, s, M_COLS_PAD), p['m_conv_w'][l], row(p['m_conv_b'][l]),
                row(gate_bias), row(p['m_norm_g'][l]))

    zeros = jnp.zeros((R_WIDTH // 8, R_WIDTH), F32)
    w_lr = jnp.concatenate([jnp.concatenate([p['r_w2'][l], zeros], axis=1),
                            jnp.concatenate([zeros, p['r_a2'][l]], axis=1)], axis=0).astype(BF16)
    parts = _rwkv_prep(proj_r.reshape(bsz, s, R_COLS), row(perm(p['r_mu'][l])), row(p['r_w0'][l]),
                       row(p['r_a0'][l]), w_lr, p['r_g2'][l].astype(BF16), row(p['r_kk'][l]),
                       row(p['r_ka'][l]), _block_diag_ones(R_WIDTH, R_HD))
    hr = _rwkv(parts, row(p['r_rk'][l]), row(p['r_gn_g'][l]), row(p['r_gn_b'][l]))

    h1 = _mix_out(hm.reshape(n, M_WIDTH), hr.reshape(n, R_WIDTH), h, p['w_out'][l].astype(BF16),
                  p['ln1_g'][l], p['ln1_b'][l])
    if debug:
        return h1, hm, hr
    return h1


def _xattn_layer(h, mem2, bsz, s, p, l):
    n, d = h.shape
    n_mem = mem2.shape[0] // bsz
    kv = _mm(mem2, p['x_wkv'][l].astype(BF16), BF16)
    return _xattn(h.reshape(bsz, s, d), kv.reshape(bsz, n_mem, 2 * d), p['x_wq'][l].astype(BF16),
                  p['x_wo'][l].astype(BF16), p['ln2_g'][l], p['ln2_b'][l]).reshape(n, d)


def kernel(x, mem, ln0_g, ln0_b, w_in, m_conv_w, m_conv_b, m_ig_b, m_fg_b, m_norm_g, r_mu, r_w0,
           r_w2, r_a0, r_a2, r_g2, r_kk, r_ka, r_rk, r_gn_g, r_gn_b, w_out, ln1_g, ln1_b, x_wq,
           x_wkv, x_wo, ln2_g, ln2_b, moe_wr, moe_br, moe_w1, moe_b1, moe_w2, moe_b2, ln3_g,
           ln3_b):
    p = dict(w_in=w_in, m_conv_w=m_conv_w, m_conv_b=m_conv_b, m_ig_b=m_ig_b, m_fg_b=m_fg_b,
             m_norm_g=m_norm_g, r_mu=r_mu, r_w0=r_w0, r_w2=r_w2, r_a0=r_a0, r_a2=r_a2, r_g2=r_g2,
             r_kk=r_kk, r_ka=r_ka, r_rk=r_rk, r_gn_g=r_gn_g, r_gn_b=r_gn_b, w_out=w_out,
             ln1_g=ln1_g, ln1_b=ln1_b, x_wq=x_wq, x_wkv=x_wkv, x_wo=x_wo, ln2_g=ln2_g,
             ln2_b=ln2_b)
    bsz, s, d = x.shape
    n = bsz * s
    h = _ln(x.reshape(n, d), ln0_g, ln0_b)
    mem2 = mem.reshape(bsz * mem.shape[1], d)
    for l in range(DEPTH):
        h = _mixer_layer(h, bsz, s, p, l)
        h = _xattn_layer(h, mem2, bsz, s, p, l)
        h = _moe(h, moe_wr[l], moe_br[l], moe_w1, moe_b1, moe_w2, moe_b2, ln3_g[l], ln3_b[l], l)
    return h.reshape(bsz, s, d)
```

```python
import functools

import jax
import jax.numpy as jnp
from jax import lax
from jax.experimental import pallas as pl
from jax.experimental.pallas import tpu as pltpu

F32 = jnp.float32
BF16 = jnp.bfloat16
HIGHEST = lax.Precision.HIGHEST

D_MODEL = 1024
DEPTH = 2
CHUNK = 64
M_WIDTH = 512
M_HEADS = 4
M_HD = 128
M_CONV = 4
R_WIDTH = 512
R_HD = 64
R_HEADS = 8
R_LR = 256
X_HEADS = 4
X_HD = 256
N_EXPERTS = 32
TOP_K = 4
D_FF = 1024
SWIGLU_LIMIT = 7.0
SWIGLU_ALPHA = 1.702
LN_EPS = 1e-5
HEAD_NORM_EPS = 1e-5
RWKV_GN_EPS = 64e-5
DN_ALPHA = (2 * DEPTH) ** 0.25

LANES = 128
M_COLS_PAD = 4 * M_WIDTH + LANES
R_COLS = 3 * R_WIDTH + R_LR
R_GROUP = 256
DMA_UNROLL = 8
MOE_ROWS = 512
VMEM_LIMIT = 56 * 1024 * 1024


def _cparams(sem, vmem=VMEM_LIMIT):
    return pltpu.CompilerParams(dimension_semantics=sem, vmem_limit_bytes=vmem)


def _dot(a, b):
    return jnp.dot(a, b, preferred_element_type=F32)


def _dot_nt(a, b):
    return lax.dot_general(a, b, (((1,), (1,)), ((), ())), preferred_element_type=F32)


def _dot_hi(a, b):
    return jnp.dot(a, b, preferred_element_type=F32, precision=HIGHEST)


def _dot_nt_hi(a, b):
    return lax.dot_general(a, b, (((1,), (1,)), ((), ())), preferred_element_type=F32,
                           precision=HIGHEST)


def _layer_norm(x, g, b):
    mu = jnp.mean(x, axis=-1, keepdims=True)
    xc = x - mu
    var = jnp.mean(xc * xc, axis=-1, keepdims=True)
    return xc * lax.rsqrt(var + LN_EPS) * g + b


def _sigmoid(x):
    return 1.0 / (1.0 + jnp.exp(-x))


def _log_sigmoid(x):
    return jnp.minimum(x, 0.0) - jnp.log1p(jnp.exp(-jnp.abs(x)))


def _ln_kernel(x_ref, g_ref, b_ref, o_ref):
    o_ref[...] = _layer_norm(x_ref[...], g_ref[...], b_ref[...])


def _ln(x, g, b, tm=512):
    n, d = x.shape
    return pl.pallas_call(
        _ln_kernel,
        grid=(n // tm,),
        in_specs=[pl.BlockSpec((tm, d), lambda i: (i, 0)),
                  pl.BlockSpec((1, d), lambda i: (0, 0)),
                  pl.BlockSpec((1, d), lambda i: (0, 0))],
        out_specs=pl.BlockSpec((tm, d), lambda i: (i, 0)),
        out_shape=jax.ShapeDtypeStruct((n, d), F32),
        compiler_params=_cparams(("parallel",)),
    )(x, g.reshape(1, d), b.reshape(1, d))


def _proj_kernel(x_ref, wm_ref, wr_ref, om_ref, or_ref):
    x = x_ref[...].astype(BF16)
    om_ref[...] = _dot(x, wm_ref[...])
    or_ref[...] = _dot(x, wr_ref[...])


def _proj(h, wm, wr, tm=512):
    n, d = h.shape
    return pl.pallas_call(
        _proj_kernel,
        grid=(n // tm,),
        in_specs=[pl.BlockSpec((tm, d), lambda i: (i, 0)),
                  pl.BlockSpec(wm.shape, lambda i: (0, 0)),
                  pl.BlockSpec(wr.shape, lambda i: (0, 0))],
        out_specs=[pl.BlockSpec((tm, wm.shape[1]), lambda i: (i, 0)),
                   pl.BlockSpec((tm, wr.shape[1]), lambda i: (i, 0))],
        out_shape=[jax.ShapeDtypeStruct((n, wm.shape[1]), F32),
                   jax.ShapeDtypeStruct((n, wr.shape[1]), F32)],
        compiler_params=_cparams(("parallel",)),
    )(h, wm, wr)


def _mm_kernel(x_ref, w_ref, o_ref):
    o_ref[...] = _dot(x_ref[...].astype(BF16), w_ref[...]).astype(o_ref.dtype)


def _mm(x, w, out_dtype, tm=512):
    n, d = x.shape
    return pl.pallas_call(
        _mm_kernel,
        grid=(n // tm,),
        in_specs=[pl.BlockSpec((tm, d), lambda i: (i, 0)),
                  pl.BlockSpec(w.shape, lambda i: (0, 0))],
        out_specs=pl.BlockSpec((tm, w.shape[1]), lambda i: (i, 0)),
        out_shape=jax.ShapeDtypeStruct((n, w.shape[1]), out_dtype),
        compiler_params=_cparams(("parallel",)),
    )(x, w)


def _mix_out_kernel(hm_ref, hr_ref, h_ref, w_ref, g_ref, b_ref, o_ref):
    mix = _dot(hm_ref[...].astype(BF16), w_ref[:M_WIDTH, :])
    mix = mix + _dot(hr_ref[...].astype(BF16), w_ref[M_WIDTH:, :])
    o_ref[...] = _layer_norm(DN_ALPHA * h_ref[...] + mix, g_ref[...], b_ref[...])


def _mix_out(hm, hr, h, w, g, b, tm=512):
    n, d = h.shape
    return pl.pallas_call(
        _mix_out_kernel,
        grid=(n // tm,),
        in_specs=[pl.BlockSpec((tm, M_WIDTH), lambda i: (i, 0)),
                  pl.BlockSpec((tm, R_WIDTH), lambda i: (i, 0)),
                  pl.BlockSpec((tm, d), lambda i: (i, 0)),
                  pl.BlockSpec(w.shape, lambda i: (0, 0)),
                  pl.BlockSpec((1, d), lambda i: (0, 0)),
                  pl.BlockSpec((1, d), lambda i: (0, 0))],
        out_specs=pl.BlockSpec((tm, d), lambda i: (i, 0)),
        out_shape=jax.ShapeDtypeStruct((n, d), F32),
        compiler_params=_cparams(("parallel",)),
    )(hm, hr, h, w, g.reshape(1, d), b.reshape(1, d))


def _mlstm_kernel(qk_ref, v_ref, o_ref, gt_ref, cw_ref, cb_ref, gb_ref, ng_ref, out_ref,
                  ct_s, n_s, m_s, tail_s, *, n_chunks):
    L = CHUNK

    @pl.when(pl.program_id(1) == 0)
    def _():
        ct_s[...] = jnp.zeros_like(ct_s)
        n_s[...] = jnp.zeros_like(n_s)
        m_s[...] = jnp.zeros_like(m_s)
        tail_s[...] = jnp.zeros_like(tail_s)

    row = lax.broadcasted_iota(jnp.int32, (L, L), 0)
    col = lax.broadcasted_iota(jnp.int32, (L, L), 1)
    causal = row >= col
    tril_f = causal.astype(F32)
    lane = lax.broadcasted_iota(jnp.int32, (1, LANES), 1)

    def chunk(c, carry):
        r0 = pl.multiple_of(c * L, L)
        x = qk_ref[0, pl.ds(r0, L), :]
        cat = jnp.concatenate([tail_s[...], x], axis=0)
        acc = cb_ref[...] + cw_ref[M_CONV - 1:M_CONV, :] * x
        for j in range(M_CONV - 1):
            acc = acc + cw_ref[j:j + 1, :] * pltpu.roll(cat, M_CONV - 1 - j, axis=0)[8:, :]
        tail_s[...] = x[L - 8:, :]
        qk = acc * _sigmoid(acc)

        gx = gt_ref[0, pl.ds(r0, L), :] + gb_ref[...]
        bc = _dot_hi(tril_f, _log_sigmoid(gx))
        rows = jnp.where(lane < M_HEADS, gx, bc).T

        for h in range(M_HEADS):
            q = qk[:, h * M_HD:(h + 1) * M_HD]
            k = qk[:, M_WIDTH + h * M_HD:M_WIDTH + (h + 1) * M_HD] * (M_HD ** -0.5)
            v = v_ref[0, pl.ds(r0, L), h * M_HD:(h + 1) * M_HD]
            ig_c = gx[:, h:h + 1]
            b_c = bc[:, M_HEADS + h:M_HEADS + h + 1]
            ig_r = rows[h:h + 1, :]
            b_r = rows[M_HEADS + h:M_HEADS + h + 1, :]
            b_last = bc[L - 1:L, M_HEADS + h:M_HEADS + h + 1]
            m_prev = m_s[h, 0:1, 0:1]
            n_prev = n_s[h, 0:1, :]
            ct_prev = ct_s[h]

            qb = q.astype(BF16)
            kb = k.astype(BF16)
            vb = v.astype(BF16)

            g_inter = b_c + m_prev
            d_mat = jnp.where(causal, b_c - b_r + ig_r, -jnp.inf)
            m_t = jnp.maximum(g_inter, jnp.max(d_mat, axis=1, keepdims=True))
            s = _dot_nt(qb, kb) * jnp.exp(d_mat - m_t)
            w_inter = jnp.exp(g_inter - m_t)
            num = _dot(s.astype(BF16), vb) + w_inter * _dot(qb, ct_prev.astype(BF16))
            den = (jnp.sum(s, axis=1, keepdims=True)
                   + w_inter * jnp.sum(q * n_prev, axis=1, keepdims=True))
            hc = num / jnp.maximum(jnp.abs(den), jnp.exp(-m_t))
            mu = jnp.mean(hc, axis=1, keepdims=True)
            hcc = hc - mu
            var = jnp.mean(hcc * hcc, axis=1, keepdims=True)
            og = o_ref[0, pl.ds(r0, L), h * M_HD:(h + 1) * M_HD]
            out = hcc * lax.rsqrt(var + HEAD_NORM_EPS) * ng_ref[:, h * M_HD:(h + 1) * M_HD]
            out_ref[0, pl.ds(r0, L), h * M_HD:(h + 1) * M_HD] = out * _sigmoid(og)

            src = b_last - b_c + ig_c
            m_loc = jnp.max(src, axis=0, keepdims=True)
            kw = k * jnp.exp(src - m_loc)
            ct_loc = _dot(kw.T.astype(BF16), vb)
            n_loc = jnp.sum(kw, axis=0, keepdims=True)
            m_new = jnp.maximum(b_last + m_prev, m_loc)
            a = jnp.exp(b_last + m_prev - m_new)
            cc = jnp.exp(m_loc - m_new)
            ct_s[h] = a * ct_prev + cc * ct_loc
            n_s[h] = jnp.broadcast_to(a * n_prev + cc * n_loc, (8, M_HD))
            m_s[h] = jnp.broadcast_to(m_new, (8, LANES))
        return carry

    lax.fori_loop(0, n_chunks, chunk, 0)


def _mlstm(proj_m, cw, cb, gb, ng, ts=512):
    bsz, s, _ = proj_m.shape
    kern = functools.partial(_mlstm_kernel, n_chunks=ts // CHUNK)
    return pl.pallas_call(
        kern,
        grid=(bsz, s // ts),
        in_specs=[pl.BlockSpec((1, ts, 2 * M_WIDTH), lambda b, t: (b, t, 0)),
                  pl.BlockSpec((1, ts, M_WIDTH), lambda b, t: (b, t, 2)),
                  pl.BlockSpec((1, ts, M_WIDTH), lambda b, t: (b, t, 3)),
                  pl.BlockSpec((1, ts, LANES), lambda b, t: (b, t, 4 * M_WIDTH // LANES)),
                  pl.BlockSpec((M_CONV, 2 * M_WIDTH), lambda b, t: (0, 0)),
                  pl.BlockSpec((1, 2 * M_WIDTH), lambda b, t: (0, 0)),
                  pl.BlockSpec((1, LANES), lambda b, t: (0, 0)),
                  pl.BlockSpec((1, M_WIDTH), lambda b, t: (0, 0))],
        out_specs=pl.BlockSpec((1, ts, M_WIDTH), lambda b, t: (b, t, 0)),
        out_shape=jax.ShapeDtypeStruct((bsz, s, M_WIDTH), F32),
        scratch_shapes=[pltpu.VMEM((M_HEADS, M_HD, M_HD), F32),
                        pltpu.VMEM((M_HEADS, 8, M_HD), F32),
                        pltpu.VMEM((M_HEADS, 8, LANES), F32),
                        pltpu.VMEM((8, 2 * M_WIDTH), F32)],
        compiler_params=_cparams(("parallel", "arbitrary")),
    )(proj_m, proj_m, proj_m, proj_m, cw, cb, gb, ng)


def _rwkv_prep_kernel(p_ref, pp_ref, mu_ref, w0_ref, a0_ref, wlr_ref, g2_ref, kkp_ref, ka_ref,
                      bd_ref, r_o, k_o, v_o, kk_o, a_o, ld_o, g_o):
    ts = p_ref.shape[1]
    p = p_ref[0]
    first = pl.program_id(1) == 0
    prev_last = jnp.where(first, 0.0, pp_ref[0, 7:8, :])
    row = lax.broadcasted_iota(jnp.int32, (ts, 1), 0)
    prev = jnp.where(row == 0, prev_last, pltpu.roll(p, 1, axis=0))
    xs = p + (prev - p) * mu_ref[...]
    rr = xs[:, :R_WIDTH]
    rk = xs[:, R_WIDTH:2 * R_WIDTH]
    rv = xs[:, 2 * R_WIDTH:3 * R_WIDTH]
    gl = xs[:, 3 * R_WIDTH:3 * R_WIDTH + LANES]
    wa = xs[:, 3 * R_WIDTH + LANES:]
    lane = lax.broadcasted_iota(jnp.int32, (1, LANES), 1)
    wa = jnp.where(lane < LANES // 2, jnp.tanh(wa), wa)
    wa_out = _dot(wa.astype(BF16), wlr_ref[...])
    w_log = _log_sigmoid(w0_ref[...] + wa_out[:, :R_WIDTH]) - 0.5
    a = _sigmoid(a0_ref[...] + wa_out[:, R_WIDTH:])
    kk = rk * kkp_ref[...]
    sq = kk * kk
    sq_hi = sq.astype(BF16)
    sq_lo = (sq - sq_hi.astype(F32)).astype(BF16)
    ss = _dot(sq_hi, bd_ref[...]) + _dot(sq_lo, bd_ref[...])
    r_o[0] = rr
    k_o[0] = rk * (1.0 + (a - 1.0) * ka_ref[...])
    v_o[0] = rv
    kk_o[0] = kk / jnp.maximum(jnp.sqrt(ss), 1e-12)
    a_o[0] = a
    ld_o[0] = -jnp.exp(w_log)
    g_o[0] = _dot(_sigmoid(gl).astype(BF16), g2_ref[...])


def _rwkv_prep(proj_r, mu, w0, a0, wlr, g2, kkp, ka, bd, ts=256):
    bsz, s, _ = proj_r.shape
    vec = lambda w: pl.BlockSpec((1, w), lambda b, t: (0, 0))
    full = lambda a: pl.BlockSpec(a.shape, lambda b, t: (0, 0))
    out_spec = pl.BlockSpec((1, ts, R_WIDTH), lambda b, t: (b, t, 0))
    out_sds = jax.ShapeDtypeStruct((bsz, s, R_WIDTH), F32)
    return pl.pallas_call(
        _rwkv_prep_kernel,
        grid=(bsz, s // ts),
        in_specs=[pl.BlockSpec((1, ts, R_COLS), lambda b, t: (b, t, 0)),
                  pl.BlockSpec((1, 8, R_COLS),
                               lambda b, t: (b, jnp.maximum(t * (ts // 8) - 1, 0), 0)),
                  vec(R_COLS), vec(R_WIDTH), vec(R_WIDTH), full(wlr), full(g2),
                  vec(R_WIDTH), vec(R_WIDTH), full(bd)],
        out_specs=[out_spec] * 7,
        out_shape=[out_sds] * 7,
        compiler_params=_cparams(("parallel", "arbitrary")),
    )(proj_r, proj_r, mu, w0, a0, wlr, g2, kkp, ka, bd)


def _group_head_sum(x, masks):
    out = jnp.sum(jnp.where(masks[-1], x, 0.0), axis=1, keepdims=True)
    for m in masks[-2::-1]:
        out = jnp.where(m, jnp.sum(jnp.where(m, x, 0.0), axis=1, keepdims=True), out)
    return out


def _rwkv_intra_kernel(r_ref, k_ref, v_ref, kk_ref, a_ref, ld_ref, g_ref, rkp_ref, gng_ref,
                       gnb_ref, ah_o, rq_o, be_o, kt_o, uh_o, y0_o, e1_o, e2_o, dl_o, *, n_chunks):
    L = CHUNK
    G = R_GROUP
    row = lax.broadcasted_iota(jnp.int32, (L, L), 0)
    col = lax.broadcasted_iota(jnp.int32, (L, L), 1)
    tril_b = (row >= col).astype(BF16)
    br = lax.broadcasted_iota(jnp.int32, (G, G), 0)
    bc = lax.broadcasted_iota(jnp.int32, (G, G), 1)
    same = (br // L) == (bc // L)
    lower_s = same & ((br % L) > (bc % L))
    lower_i = same & ((br % L) >= (bc % L))
    eye = (br == bc).astype(F32)
    lane = lax.broadcasted_iota(jnp.int32, (1, G), 1)
    masks = [(lane // R_HD) == h for h in range(G // R_HD)]

    def stack(x):
        return jnp.concatenate([jnp.where(m, x, 0.0) for m in masks], axis=0).astype(BF16)

    def collapse(x):
        out = x[0:L]
        for h in range(1, G // R_HD):
            out = out + x[h * L:(h + 1) * L]
        return out

    chains = []
    for ci in range(n_chunks):
        rows = slice(ci * L, (ci + 1) * L)
        for gi in range(R_WIDTH // G):
            sl = slice(gi * G, (gi + 1) * G)
            r = r_ref[0, rows, sl]
            k = k_ref[0, rows, sl]
            v = v_ref[0, rows, sl]
            kk = kk_ref[0, rows, sl]
            a = a_ref[0, rows, sl]
            ld = ld_ref[0, rows, sl]

            ld_hi = ld.astype(BF16)
            ld_lo = (ld - ld_hi.astype(F32)).astype(BF16)
            cum = _dot(tril_b, ld_hi) + _dot(tril_b, ld_lo)
            dec = jnp.exp(cum)
            inv = jnp.exp(-cum)
            al = -kk * jnp.exp(cum - ld)
            be = kk * a * inv
            kt = k * inv
            rt = r * dec
            dl = dec[L - 1:L, :]

            al4 = stack(al)
            v4 = stack(v)
            big = _dot_nt(jnp.concatenate([al4, stack(rt)], axis=0),
                          jnp.concatenate([stack(be), stack(kt)], axis=0))
            ab = jnp.where(lower_s, big[:G, :G], 0.0)
            ak = jnp.where(lower_s, big[:G, G:], 0.0)
            rb = jnp.where(lower_i, big[G:, :G], 0.0).astype(BF16)
            rk = jnp.where(lower_i, big[G:, G:], 0.0).astype(BF16)
            chains.append([rows, sl, ci, r, k, v, be, kt, rt, dl, al4, v4, ak, rb, rk,
                           eye + ab, ab])

    for _ in range(5):
        for c in chains:
            xpb = c[-1].astype(BF16)
            c[-1] = _dot(xpb, xpb)
            c[-2] = c[-2] + _dot(c[-2].astype(BF16), c[-1].astype(BF16))

    akvs = [_dot(c[12].astype(BF16), c[11]).astype(BF16) for c in chains]
    wbs = [c[15].astype(BF16) for c in chains]
    ahss = [_dot(wb, c[10]) for wb, c in zip(wbs, chains)]
    uhss = [_dot(wb, akv) for wb, akv in zip(wbs, akvs)]
    rqss = [_dot(c[13], ahs.astype(BF16)) for c, ahs in zip(chains, ahss)]
    y0ss = [_dot(c[13], uhs.astype(BF16)) + _dot(c[14], c[11]) for c, uhs in zip(chains, uhss)]

    for c, ahs, uhs, rqs, y0s in zip(chains, ahss, uhss, rqss, y0ss):
            rows, sl, ci, r, k, v, be, kt, rt, dl = c[:10]

            ah_o[0, rows, sl] = collapse(ahs).astype(BF16)
            rq_o[0, rows, sl] = (rt + collapse(rqs)).astype(BF16)
            be_o[0, rows, sl] = (be * dl).astype(BF16)
            kt_o[0, rows, sl] = (kt * dl).astype(BF16)
            uh_o[0, rows, sl] = collapse(uhs)
            y0_o[0, rows, sl] = collapse(y0s)
            g = g_ref[0, rows, sl]
            bonus = _group_head_sum(r * k * rkp_ref[:, sl], masks) * v
            e1_o[0, rows, sl] = g * gng_ref[:, sl]
            e2_o[0, rows, sl] = (gnb_ref[:, sl] + bonus) * g
            dl_o[0, ci * 8:(ci + 1) * 8, sl] = jnp.broadcast_to(dl, (8, G))


def _rwkv_seq_kernel(ah_ref, rq_ref, be_ref, kt_ref, v_ref, uh_ref, y0_ref, e1_ref, e2_ref, dl_ref,
                     out_ref, z_s, *, n_chunks):
    L = CHUNK
    G = R_GROUP

    @pl.when(pl.program_id(1) == 0)
    def _():
        z_s[...] = jnp.zeros_like(z_s)

    zr = lax.broadcasted_iota(jnp.int32, (G, G), 0) // R_HD
    zc = lax.broadcasted_iota(jnp.int32, (G, G), 1) // R_HD
    same_head = zr == zc
    lane = lax.broadcasted_iota(jnp.int32, (1, G), 1)
    masks = [(lane // R_HD) == h for h in range(G // R_HD)]

    def chunk(c, carry):
        r0 = pl.multiple_of(c * L, L)
        d0 = pl.multiple_of(c * 8, 8)
        for gi in range(R_WIDTH // G):
            sl = slice(gi * G, (gi + 1) * G)
            zt = z_s[gi]
            lhs = jnp.concatenate([ah_ref[0, pl.ds(r0, L), sl], rq_ref[0, pl.ds(r0, L), sl]],
                                  axis=0)
            pr = _dot_nt(lhs, zt.astype(BF16))
            u = pr[:L] + uh_ref[0, pl.ds(r0, L), sl]
            y = pr[L:] + y0_ref[0, pl.ds(r0, L), sl]
            uv = jnp.concatenate([u, v_ref[0, pl.ds(r0, L), sl]], axis=0)
            bk = jnp.concatenate([be_ref[0, pl.ds(r0, L), sl], kt_ref[0, pl.ds(r0, L), sl]],
                                 axis=0)
            upd = _dot(uv.T.astype(BF16), bk)
            z_s[gi] = zt * dl_ref[0, pl.ds(d0, 1), sl] + jnp.where(same_head, upd, 0.0)

            mu = _group_head_sum(y, masks) * (1.0 / R_HD)
            yc = y - mu
            var = _group_head_sum(yc * yc, masks) * (1.0 / R_HD)
            out_ref[0, pl.ds(r0, L), sl] = (yc * lax.rsqrt(var + RWKV_GN_EPS)
                                            * e1_ref[0, pl.ds(r0, L), sl]
                                            + e2_ref[0, pl.ds(r0, L), sl])
        return carry

    lax.fori_loop(0, n_chunks, chunk, 0)


def _rwkv(parts, rkp, gng, gnb, ts_intra=256, ts_seq=512):
    bsz, s, _ = parts[0].shape
    vec = pl.BlockSpec((1, R_WIDTH), lambda b, t: (0, 0))
    blk = pl.BlockSpec((1, ts_intra, R_WIDTH), lambda b, t: (b, t, 0))
    sds = lambda dt: jax.ShapeDtypeStruct((bsz, s, R_WIDTH), dt)
    ah, rq, be, kt, uh, y0, e1, e2, dl = pl.pallas_call(
        functools.partial(_rwkv_intra_kernel, n_chunks=ts_intra // CHUNK),
        grid=(bsz, s // ts_intra),
        in_specs=[blk] * 7 + [vec] * 3,
        out_specs=[blk] * 8 + [pl.BlockSpec((1, ts_intra // 8, R_WIDTH), lambda b, t: (b, t, 0))],
        out_shape=[sds(BF16)] * 4 + [sds(F32)] * 4
        + [jax.ShapeDtypeStruct((bsz, s // 8, R_WIDTH), F32)],
        compiler_params=_cparams(("parallel", "parallel")),
    )(*parts, rkp, gng, gnb)
    blk = pl.BlockSpec((1, ts_seq, R_WIDTH), lambda b, t: (b, t, 0))
    return pl.pallas_call(
        functools.partial(_rwkv_seq_kernel, n_chunks=ts_seq // CHUNK),
        grid=(bsz, s // ts_seq),
        in_specs=[blk] * 9 + [pl.BlockSpec((1, ts_seq // 8, R_WIDTH), lambda b, t: (b, t, 0))],
        out_specs=blk,
        out_shape=sds(F32),
        scratch_shapes=[pltpu.VMEM((R_WIDTH // R_GROUP, R_GROUP, R_GROUP), F32)],
        compiler_params=_cparams(("parallel", "arbitrary")),
    )(ah, rq, be, kt, parts[2], uh, y0, e1, e2, dl)


def _xattn_kernel(h_ref, kv_ref, wq_ref, wo_ref, g_ref, b_ref, o_ref):
    hq = h_ref[0]
    q = _dot(hq.astype(BF16), wq_ref[...])
    acc = jnp.zeros_like(hq)
    for hd in range(X_HEADS):
        sl = slice(hd * X_HD, (hd + 1) * X_HD)
        kh = kv_ref[0, :, sl]
        vh = kv_ref[0, :, D_MODEL + hd * X_HD:D_MODEL + (hd + 1) * X_HD]
        s = _dot_nt(q[:, sl].astype(BF16), kh) * (X_HD ** -0.5)
        e = jnp.exp(s - jnp.max(s, axis=1, keepdims=True))
        p = e / jnp.sum(e, axis=1, keepdims=True)
        oh = _dot(p.astype(BF16), vh)
        acc = acc + _dot(oh.astype(BF16), wo_ref[sl, :])
    o_ref[0] = _layer_norm(DN_ALPHA * hq + acc, g_ref[...], b_ref[...])


def _xattn(h3, kv3, wq, wo, g, b, tq=512):
    bsz, s, d = h3.shape
    m = kv3.shape[1]
    return pl.pallas_call(
        _xattn_kernel,
        grid=(bsz, s // tq),
        in_specs=[pl.BlockSpec((1, tq, d), lambda i, t: (i, t, 0)),
                  pl.BlockSpec((1, m, 2 * d), lambda i, t: (i, 0, 0)),
                  pl.BlockSpec(wq.shape, lambda i, t: (0, 0)),
                  pl.BlockSpec(wo.shape, lambda i, t: (0, 0)),
                  pl.BlockSpec((1, d), lambda i, t: (0, 0)),
                  pl.BlockSpec((1, d), lambda i, t: (0, 0))],
        out_specs=pl.BlockSpec((1, tq, d), lambda i, t: (i, t, 0)),
        out_shape=jax.ShapeDtypeStruct((bsz, s, d), F32),
        compiler_params=_cparams(("parallel", "parallel")),
    )(h3, kv3, wq, wo, g.reshape(1, d), b.reshape(1, d))


def _router_kernel(h_ref, wr_ref, br_ref, idx_o, gate_o, rank_o, meta_o, blk_o, cnt_s,
                   *, n_blocks):
    tm = h_ref.shape[0]
    step = pl.program_id(0)

    @pl.when(step == 0)
    def _():
        cnt_s[...] = jnp.zeros_like(cnt_s)

    lane = lax.broadcasted_iota(jnp.int32, (1, LANES), 1)
    logits = _dot_hi(h_ref[...], wr_ref[...]) + br_ref[...]
    logits = jnp.where(lane < N_EXPERTS, logits, -jnp.inf)
    vals, sels = [], []
    idx_out = jnp.zeros((tm, LANES), jnp.int32)
    for j in range(TOP_K):
        mx = jnp.max(logits, axis=1, keepdims=True)
        ij = jnp.min(jnp.where(logits == mx, lane, LANES), axis=1, keepdims=True)
        sel = lane == ij
        vals.append(mx)
        sels.append(sel)
        idx_out = jnp.where(lane == j, ij, idx_out)
        logits = jnp.where(sel, -jnp.inf, logits)
    exps = [jnp.exp(vj - vals[0]) for vj in vals]
    denom = exps[0] + exps[1] + exps[2] + exps[3]
    gate_out = jnp.zeros((tm, LANES), F32)
    for j in range(TOP_K):
        gate_out = jnp.where(lane == j, exps[j] / denom, gate_out)

    cnt = jnp.zeros((tm, LANES), F32)
    for sel in sels:
        cnt = cnt + sel.astype(F32)
    row = lax.broadcasted_iota(jnp.int32, (tm, tm), 0)
    col = lax.broadcasted_iota(jnp.int32, (tm, tm), 1)
    before = (row > col).astype(BF16)
    excl = _dot(before, cnt.astype(BF16)) + cnt_s[0:1, :]
    rank_out = jnp.zeros((tm, LANES), jnp.int32)
    for j in range(TOP_K):
        rj = jnp.sum(jnp.where(sels[j], excl, 0.0), axis=1, keepdims=True)
        rank_out = jnp.where(lane == j, rj.astype(jnp.int32), rank_out)
    total = cnt_s[0:1, :] + jnp.sum(cnt, axis=0, keepdims=True)
    cnt_s[...] = jnp.broadcast_to(total, cnt_s.shape)

    idx_o[...] = idx_out
    gate_o[...] = gate_out
    rank_o[...] = rank_out

    padded = jnp.floor((total + (MOE_ROWS - 1)) * (1.0 / MOE_ROWS)) * MOE_ROWS
    er = lax.broadcasted_iota(jnp.int32, (LANES, LANES), 0)
    ec = lax.broadcasted_iota(jnp.int32, (LANES, LANES), 1)
    pad_end = _dot_hi(padded, (er <= ec).astype(F32))
    pad_start = pad_end - padded
    meta = jnp.where(lax.broadcasted_iota(jnp.int32, (8, LANES), 0) == 0, pad_start, pad_end)
    meta_o[...] = meta.astype(jnp.int32)
    brow = lax.broadcasted_iota(jnp.int32, (n_blocks, LANES), 0).astype(F32) * MOE_ROWS
    done = jnp.where((lane < N_EXPERTS) & (pad_end <= brow), 1.0, 0.0)
    be = jnp.minimum(jnp.sum(done, axis=1, keepdims=True), N_EXPERTS - 1.0)
    blk_o[...] = jnp.broadcast_to(be, (n_blocks, LANES)).astype(jnp.int32)


def _router(h, wr, br, n_blocks, tm=512):
    n, d = h.shape
    kern = functools.partial(_router_kernel, n_blocks=n_blocks)
    tok = pl.BlockSpec((tm, LANES), lambda i: (i, 0))
    return pl.pallas_call(
        kern,
        grid=(n // tm,),
        in_specs=[pl.BlockSpec((tm, d), lambda i: (i, 0)),
                  pl.BlockSpec(wr.shape, lambda i: (0, 0)),
                  pl.BlockSpec((1, LANES), lambda i: (0, 0))],
        out_specs=[tok, tok, tok,
                   pl.BlockSpec((8, LANES), lambda i: (0, 0)),
                   pl.BlockSpec((n_blocks, LANES), lambda i: (0, 0))],
        out_shape=[jax.ShapeDtypeStruct((n, LANES), jnp.int32),
                   jax.ShapeDtypeStruct((n, LANES), F32),
                   jax.ShapeDtypeStruct((n, LANES), jnp.int32),
                   jax.ShapeDtypeStruct((8, LANES), jnp.int32),
                   jax.ShapeDtypeStruct((n_blocks, LANES), jnp.int32)],
        scratch_shapes=[pltpu.VMEM((8, LANES), F32)],
        compiler_params=_cparams(("arbitrary",)),
    )(h, wr, br)


def _dispatch_kernel(idx_ref, rank_ref, start_ref, x_ref, buf_in, buf_ref, sem):
    del buf_in
    tm = x_ref.shape[0]

    def copies(n):
        out = []
        for j in range(TOP_K):
            i = n * TOP_K + j
            dest = start_ref[idx_ref[i]] + rank_ref[i]
            out.append(pltpu.make_async_copy(x_ref.at[pl.ds(n, 1), :],
                                             buf_ref.at[pl.ds(dest, 1), :], sem))
        return out

    def issue(n, c):
        for cp in copies(n):
            cp.start()
        return c

    def drain(n, c):
        for cp in copies(n):
            cp.wait()
        return c

    lax.fori_loop(0, tm, issue, 0, unroll=DMA_UNROLL)
    lax.fori_loop(0, tm, drain, 0, unroll=DMA_UNROLL)


def _dispatch(h, idx_flat, rank_flat, pad_start, n_rows, tm=512):
    n, d = h.shape
    flat = pl.BlockSpec((tm * TOP_K,), lambda i: (i,), memory_space=pltpu.SMEM)
    return pl.pallas_call(
        _dispatch_kernel,
        grid=(n // tm,),
        in_specs=[flat, flat,
                  pl.BlockSpec(memory_space=pltpu.SMEM),
                  pl.BlockSpec((tm, d), lambda i: (i, 0)),
                  pl.BlockSpec(memory_space=pl.ANY)],
        out_specs=pl.BlockSpec(memory_space=pl.ANY),
        out_shape=jax.ShapeDtypeStruct((n_rows, d), F32),
        scratch_shapes=[pltpu.SemaphoreType.DMA(())],
        input_output_aliases={4: 0},
        compiler_params=_cparams(("arbitrary",)),
    )(idx_flat, rank_flat, pad_start, h, jnp.zeros((n_rows, d), F32))


def _expert_kernel(be_ref, nb_ref, x_ref, w1_ref, b1_ref, w2_ref, b2_ref, y_ref, w1_s, w2_s):
    i = pl.program_id(0)
    used = i < nb_ref[0]
    fresh = jnp.logical_or(i == 0, be_ref[i] != be_ref[jnp.maximum(i - 1, 0)])

    @pl.when(jnp.logical_not(used))
    def _():
        y_ref[...] = jnp.zeros_like(y_ref)

    @pl.when(jnp.logical_and(used, fresh))
    def _():
        w1_s[...] = w1_ref[0, 0].astype(BF16)
        w2_s[...] = w2_ref[0, 0].astype(BF16)

    @pl.when(used)
    def _():
        hu = _dot(x_ref[...].astype(BF16), w1_s[...]) + b1_ref[0, 0]
        gate = jnp.minimum(hu[:, :D_FF], SWIGLU_LIMIT)
        up = jnp.clip(hu[:, D_FF:], -SWIGLU_LIMIT, SWIGLU_LIMIT)
        glu = gate * _sigmoid(gate * SWIGLU_ALPHA)
        y_ref[...] = _dot(((up + 1.0) * glu).astype(BF16), w2_s[...]) + b2_ref[0, 0]


def _experts(buf, block_e, n_used, w1, b1, w2, b2, l):
    n_rows, d = buf.shape
    n_blocks = n_rows // MOE_ROWS
    last = lambda i, nb: jnp.maximum(jnp.minimum(i, nb[0] - 1), 0)
    rows = lambda i, be, nb: (last(i, nb), 0)
    wsel = lambda i, be, nb: (l, be[last(i, nb)], 0, 0)
    return pl.pallas_call(
        _expert_kernel,
        grid_spec=pltpu.PrefetchScalarGridSpec(
            num_scalar_prefetch=2,
            grid=(n_blocks,),
            in_specs=[pl.BlockSpec((MOE_ROWS, d), rows),
                      pl.BlockSpec((1, 1, d, 2 * D_FF), wsel),
                      pl.BlockSpec((1, 1, 1, 2 * D_FF), wsel),
                      pl.BlockSpec((1, 1, D_FF, d), wsel),
                      pl.BlockSpec((1, 1, 1, d), wsel)],
            out_specs=pl.BlockSpec((MOE_ROWS, d), lambda i, be, nb: (i, 0)),
            scratch_shapes=[pltpu.VMEM((d, 2 * D_FF), BF16), pltpu.VMEM((D_FF, d), BF16)]),
        out_shape=jax.ShapeDtypeStruct((n_rows, d), F32),
        compiler_params=_cparams(("arbitrary",)),
    )(block_e, n_used, buf, w1, b1[:, :, None, :], w2, b2[:, :, None, :])


def _combine_kernel(idx_ref, rank_ref, start_ref, gate_ref, h_ref, g_ref, b_ref, y_hbm, o_ref,
                    gbuf, sem):
    tm = h_ref.shape[0]

    def copies(n):
        out = []
        for j in range(TOP_K):
            i = n * TOP_K + j
            src = start_ref[idx_ref[i]] + rank_ref[i]
            out.append(pltpu.make_async_copy(y_hbm.at[pl.ds(src, 1), :],
                                             gbuf.at[j, pl.ds(n, 1), :], sem))
        return out

    def issue(n, c):
        for cp in copies(n):
            cp.start()
        return c

    def drain(n, c):
        for cp in copies(n):
            cp.wait()
        return c

    lax.fori_loop(0, tm, issue, 0, unroll=DMA_UNROLL)
    lax.fori_loop(0, tm, drain, 0, unroll=DMA_UNROLL)
    moe = gate_ref[:, 0:1] * gbuf[0]
    for j in range(1, TOP_K):
        moe = moe + gate_ref[:, j:j + 1] * gbuf[j]
    o_ref[...] = _layer_norm(DN_ALPHA * h_ref[...] + moe, g_ref[...], b_ref[...])


def _combine(h, y_buf, idx_flat, rank_flat, pad_start, gates, g, b, tm=512):
    n, d = h.shape
    flat = pl.BlockSpec((tm * TOP_K,), lambda i: (i,), memory_space=pltpu.SMEM)
    return pl.pallas_call(
        _combine_kernel,
        grid=(n // tm,),
        in_specs=[flat, flat,
                  pl.BlockSpec(memory_space=pltpu.SMEM),
                  pl.BlockSpec((tm, LANES), lambda i: (i, 0)),
                  pl.BlockSpec((tm, d), lambda i: (i, 0)),
                  pl.BlockSpec((1, d), lambda i: (0, 0)),
                  pl.BlockSpec((1, d), lambda i: (0, 0)),
                  pl.BlockSpec(memory_space=pl.ANY)],
        out_specs=pl.BlockSpec((tm, d), lambda i: (i, 0)),
        out_shape=jax.ShapeDtypeStruct((n, d), F32),
        scratch_shapes=[pltpu.VMEM((TOP_K, tm, d), F32), pltpu.SemaphoreType.DMA(())],
        compiler_params=_cparams(("arbitrary",)),
    )(idx_flat, rank_flat, pad_start, gates, h, g.reshape(1, d), b.reshape(1, d), y_buf)


def _moe(h, wr, br, w1, b1, w2, b2, g, b, l):
    n, d = h.shape
    n_blocks = n * TOP_K // MOE_ROWS + N_EXPERTS
    wr_pad = jnp.pad(wr, ((0, 0), (0, LANES - N_EXPERTS)))
    br_pad = jnp.pad(br, (0, LANES - N_EXPERTS)).reshape(1, LANES)
    idx, gates, rank, meta, blk = _router(h, wr_pad, br_pad, n_blocks)
    idx_flat = idx[:, :TOP_K].reshape(-1)
    rank_flat = rank[:, :TOP_K].reshape(-1)
    pad_start = meta[0, :N_EXPERTS]
    n_used = meta[1, N_EXPERTS - 1:N_EXPERTS] // MOE_ROWS
    buf = _dispatch(h, idx_flat, rank_flat, pad_start, n_blocks * MOE_ROWS)
    y_buf = _experts(buf, blk[:, 0], n_used, w1, b1, w2, b2, l)
    return _combine(h, y_buf, idx_flat, rank_flat, pad_start, gates, g, b)


def _block_diag_ones(width, head):
    r = jnp.arange(width) // head
    return (r[:, None] == r[None, :]).astype(BF16)


def _mixer_layer(h, bsz, s, p, l, debug=False):
    m_cols = 4 * M_WIDTH + 2 * M_HEADS
    rw = 3 * R_WIDTH
    row = lambda a: a.reshape(1, -1)
    n = bsz * s
    wl = p['w_in'][l]
    w_m = jnp.pad(wl[:, :m_cols], ((0, 0), (0, M_COLS_PAD - m_cols))).astype(BF16)
    perm = lambda a: jnp.concatenate([a[..., :rw], a[..., rw + 128:], a[..., rw:rw + 128]], axis=-1)
    w_r = perm(wl[:, m_cols:]).astype(BF16)
    proj_m, proj_r = _proj(h, w_m, w_r)

    gate_bias = jnp.pad(jnp.concatenate([p['m_ig_b'][l], p['m_fg_b'][l]]),
                        (0, LANES - 2 * M_HEADS))
    hm = _mlstm(proj_m.reshape(bsz, s, M_COLS_PAD), p['m_conv_w'][l], row(p['m_conv_b'][l]),
                row(gate_bias), row(p['m_norm_g'][l]))

    zeros = jnp.zeros((R_WIDTH // 8, R_WIDTH), F32)
    w_lr = jnp.concatenate([jnp.concatenate([p['r_w2'][l], zeros], axis=1),
                            jnp.concatenate([zeros, p['r_a2'][l]], axis=1)], axis=0).astype(BF16)
    parts = _rwkv_prep(proj_r.reshape(bsz, s, R_COLS), row(perm(p['r_mu'][l])), row(p['r_w0'][l]),
                       row(p['r_a0'][l]), w_lr, p['r_g2'][l].astype(BF16), row(p['r_kk'][l]),
                       row(p['r_ka'][l]), _block_diag_ones(R_WIDTH, R_HD))
    hr = _rwkv(parts, row(p['r_rk'][l]), row(p['r_gn_g'][l]), row(p['r_gn_b'][l]))

    h1 = _mix_out(hm.reshape(n, M_WIDTH), hr.reshape(n, R_WIDTH), h, p['w_out'][l].astype(BF16),
                  p['ln1_g'][l], p['ln1_b'][l])
    if debug:
        return h1, hm, hr
    return h1


def _xattn_layer(h, mem2, bsz, s, p, l):
    n, d = h.shape
    n_mem = mem2.shape[0] // bsz
    kv = _mm(mem2, p['x_wkv'][l].astype(BF16), BF16)
    return _xattn(h.reshape(bsz, s, d), kv.reshape(bsz, n_mem, 2 * d), p['x_wq'][l].astype(BF16),
                  p['x_wo'][l].astype(BF16), p['ln2_g'][l], p['ln2_b'][l]).reshape(n, d)


def kernel(x, mem, ln0_g, ln0_b, w_in, m_conv_w, m_conv_b, m_ig_b, m_fg_b, m_norm_g, r_mu, r_w0,
           r_w2, r_a0, r_a2, r_g2, r_kk, r_ka, r_rk, r_gn_g, r_gn_b, w_out, ln1_g, ln1_b, x_wq,
           x_wkv, x_wo, ln2_g, ln2_b, moe_wr, moe_br, moe_w1, moe_b1, moe_w2, moe_b2, ln3_g,
           ln3_b):
    p = dict(w_in=w_in, m_conv_w=m_conv_w, m_conv_b=m_conv_b, m_ig_b=m_ig_b, m_fg_b=m_fg_b,
             m_norm_g=m_norm_g, r_mu=r_mu, r_w0=r_w0, r_w2=r_w2, r_a0=r_a0, r_a2=r_a2, r_g2=r_g2,
             r_kk=r_kk, r_ka=r_ka, r_rk=r_rk, r_gn_g=r_gn_g, r_gn_b=r_gn_b, w_out=w_out,
             ln1_g=ln1_g, ln1_b=ln1_b, x_wq=x_wq, x_wkv=x_wkv, x_wo=x_wo, ln2_g=ln2_g,
             ln2_b=ln2_b)
    bsz, s, d = x.shape
    n = bsz * s
    h = _ln(x.reshape(n, d), ln0_g, ln0_b)
    mem2 = mem.reshape(bsz * mem.shape[1], d)
    for l in range(DEPTH):
        h = _mixer_layer(h, bsz, s, p, l)
        h = _xattn_layer(h, mem2, bsz, s, p, l)
        h = _moe(h, moe_wr[l], moe_br[l], moe_w1, moe_b1, moe_w2, moe_b2, ln3_g[l], ln3_b[l], l)
    return h.reshape(bsz, s, d)
```
